```python
import jax, jax.numpy as jnp
from jax import lax
import numpy as np

D_MODEL = 2048
BATCH = 2
SEQ = 4096
DEPTH = 1
DEC_BATCH = 2
DEC_SEQ = 8192
PAST_LEN = 128

HEAD_DIM = 128
N_HEADS = 16
N_KV_HEADS = 4
WINDOW = 128
BLOCK = 128
ROPE_THETA = 10000.0
SG_GROUPS = 16
SG_WIDTH = 2048
SG_CHUNK = 128
N_EXPERTS = 32
TOP_K = 4
D_FF = 2048
SWIGLU_LIMIT = 7.0
SWIGLU_ALPHA = 1.702
PLE_DIM = 256
N_BRANCHES = 2
MOE_BLOCK = 128
EPS = 1e-6
NEG_INF = -1e30

Q_COLS = N_HEADS * HEAD_DIM
KV_COLS = N_KV_HEADS * HEAD_DIM
IN_COLS = Q_COLS + 2 * KV_COLS + 2 * SG_WIDTH + N_BRANCHES * D_MODEL
SPLITS = [Q_COLS, Q_COLS + KV_COLS, Q_COLS + 2 * KV_COLS, Q_COLS + 2 * KV_COLS + 2 * SG_WIDTH]

kernel_name = "hybrid_window_gqa_gmlp_moe_encoder"


def rmsnorm(x, g):
    xf = x.astype(jnp.float32)
    y = xf * lax.rsqrt(jnp.mean(xf * xf, axis=-1, keepdims=True) + EPS) * g.astype(jnp.float32)
    return y.astype(x.dtype)


def rope(x):
    s = x.shape[1]
    half = HEAD_DIM // 2
    inv_freq = ROPE_THETA ** (-jnp.arange(half, dtype=jnp.float32) / half)
    ang = jnp.arange(s, dtype=jnp.float32)[:, None] * inv_freq[None, :]
    cos = jnp.cos(ang)[None, :, None, :]
    sin = jnp.sin(ang)[None, :, None, :]
    xf = x.astype(jnp.float32)
    x1, x2 = xf[..., :half], xf[..., half:]
    return jnp.concatenate([x1 * cos - x2 * sin, x2 * cos + x1 * sin], axis=-1).astype(x.dtype)


def window_attention(q, k, v, sinks):
    b, s = q.shape[0], q.shape[1]
    nb = s // BLOCK
    grp = N_HEADS // N_KV_HEADS
    qb = q.reshape(b, nb, BLOCK, N_KV_HEADS, grp, HEAD_DIM)

    def band(t):
        tp = jnp.pad(t, ((0, 0), (BLOCK, BLOCK), (0, 0), (0, 0))).reshape(b, nb + 2, BLOCK, N_KV_HEADS, HEAD_DIM)
        return jnp.concatenate([tp[:, :-2], tp[:, 1:-1], tp[:, 2:]], axis=2)

    kb, vb = band(k), band(v)
    scores = jnp.einsum('bnqhgd,bnkhd->bnhgqk', qb, kb, preferred_element_type=jnp.float32) * (HEAD_DIM ** -0.5)
    qpos = jnp.arange(nb)[:, None] * BLOCK + jnp.arange(BLOCK)[None, :]
    kpos = jnp.arange(nb)[:, None] * BLOCK - BLOCK + jnp.arange(3 * BLOCK)[None, :]
    valid = (jnp.abs(qpos[:, :, None] - kpos[:, None, :]) <= WINDOW) & (kpos >= 0)[:, None, :] & (kpos < s)[:, None, :]
    scores = jnp.where(valid[None, :, None, None], scores, NEG_INF)
    sink = sinks.astype(jnp.float32).reshape(N_KV_HEADS, grp)[None, None, :, :, None, None]
    m = jnp.maximum(jnp.max(scores, axis=-1, keepdims=True), sink)
    pr = jnp.exp(scores - m)
    probs = pr / (jnp.sum(pr, axis=-1, keepdims=True) + jnp.exp(sink - m))
    out = jnp.einsum('bnhgqk,bnkhd->bnqhgd', probs.astype(v.dtype), vb)
    return out.reshape(b, s, N_HEADS * HEAD_DIM)


def spatial_gating(z, g_v, w_s, b_s):
    b, s = z.shape[0], z.shape[1]
    z = jax.nn.gelu(z)
    u, v = z[..., :SG_WIDTH], z[..., SG_WIDTH:]
    v = rmsnorm(v, g_v)
    nc = s // SG_CHUNK
    vc = v.reshape(b, nc, SG_CHUNK, SG_GROUPS, SG_WIDTH // SG_GROUPS)
    mixed = jnp.einsum('gpq,bcqgd->bcpgd', w_s, vc) + jnp.transpose(b_s)[None, None, :, :, None]
    return u * mixed.reshape(b, s, SG_WIDTH)


def moe(h, w_router, b_router, w_gu, b_gu, w_down, b_down):
    n, d = h.shape
    logits = h.astype(jnp.float32) @ w_router.astype(jnp.float32) + b_router.astype(jnp.float32)
    top_vals, top_idx = lax.top_k(logits, TOP_K)
    weights = jax.nn.softmax(top_vals, axis=-1)
    flat_e = top_idx.reshape(-1)
    order = jnp.argsort(flat_e)
    e_sorted = flat_e[order]
    tok_sorted = (order // TOP_K).astype(jnp.int32)
    w_sorted = weights.reshape(-1)[order]
    counts = jnp.bincount(flat_e, length=N_EXPERTS)
    padded = (counts + MOE_BLOCK - 1) // MOE_BLOCK * MOE_BLOCK
    start = jnp.cumsum(counts) - counts
    padded_end = jnp.cumsum(padded)
    padded_start = padded_end - padded
    dest = padded_start[e_sorted] + (jnp.arange(n * TOP_K) - start[e_sorted])
    n_blocks = (n * TOP_K + N_EXPERTS * (MOE_BLOCK - 1) + MOE_BLOCK - 1) // MOE_BLOCK
    row_tok = jnp.zeros((n_blocks * MOE_BLOCK,), jnp.int32).at[dest].set(tok_sorted)
    block_e = jnp.minimum(jnp.searchsorted(padded_end, jnp.arange(n_blocks) * MOE_BLOCK, side='right'), N_EXPERTS - 1)

    def expert_block(args):
        toks, e = args
        xb = h[toks]
        gu = xb @ w_gu[e] + b_gu[e]
        gate = jnp.minimum(gu[:, :D_FF], SWIGLU_LIMIT)
        up = jnp.clip(gu[:, D_FF:], -SWIGLU_LIMIT, SWIGLU_LIMIT)
        act = (up + 1.0) * gate * jax.nn.sigmoid(SWIGLU_ALPHA * gate)
        return act @ w_down[e] + b_down[e]

    rows = lax.map(expert_block, (row_tok.reshape(n_blocks, MOE_BLOCK), block_e)).reshape(-1, d)
    return jnp.zeros_like(h).at[tok_sorted].add(rows[dest] * w_sorted[:, None].astype(h.dtype))


def encoder_layer(x, p, g_mix, w_in, q_norm, k_norm, sinks, g_sg_v, w_s, b_s, w_branch, w_o,
                  g_ffn, w_router, b_router, w_gu, b_gu, w_down, b_down, g_ple, w_ple_gate, w_ple_proj):
    b, s, d = x.shape
    h = rmsnorm(x, g_mix)
    proj = h @ w_in
    q, k, v, z, gate_logits = jnp.split(proj, SPLITS, axis=-1)
    q = rope(rmsnorm(q.reshape(b, s, N_HEADS, HEAD_DIM), q_norm))
    k = rope(rmsnorm(k.reshape(b, s, N_KV_HEADS, HEAD_DIM), k_norm))
    v = v.reshape(b, s, N_KV_HEADS, HEAD_DIM)
    attn = window_attention(q, k, v, sinks)
    sg = spatial_gating(z, g_sv := g_sg_v, w_s, b_s) if False else spatial_gating(z, g_sg_v, w_s, b_s)
    branches = jnp.stack([attn, sg], axis=0)
    proj_b = jnp.einsum('nbsc,ncd->nbsd', branches, w_branch)
    gates = jax.nn.sigmoid(gate_logits).reshape(b, s, N_BRANCHES, d)
    merged = jnp.einsum('bsnd,nbsd->bsd', gates, proj_b)
    x = x + merged @ w_o
    x = x + moe(rmsnorm(x, g_ffn).reshape(-1, d), w_router, b_router, w_gu, b_gu, w_down, b_down).reshape(b, s, d)
    ple_gate = jax.nn.sigmoid(rmsnorm(x, g_ple) @ w_ple_gate)
    return x + ple_gate * (p @ w_ple_proj)


def trunk(x, p, g_mix, w_in, q_norm, k_norm, sinks, g_sg_v, w_s, b_s, w_branch, w_o,
          g_ffn, w_router, b_router, w_gu, b_gu, w_down, b_down, g_ple, w_ple_gate, w_ple_proj):
    for l in range(DEPTH):
        x = encoder_layer(x, p[l], g_mix[l], w_in[l], q_norm[l], k_norm[l], sinks[l], g_sg_v[l], w_s[l], b_s[l],
                          w_branch[l], w_o[l], g_ffn[l], w_router[l], b_router[l], w_gu[l], b_gu[l],
                          w_down[l], b_down[l], g_ple[l], w_ple_gate[l], w_ple_proj[l])
    return x


def setup_inputs(seed: int = 0) -> dict:
    key = jax.random.key(seed)
    ks = jax.random.split(key, 26)
    f32 = jnp.float32
    nrm = lambda k, shape, scale: jax.random.normal(k, shape, f32) * scale
    gain = lambda k, shape: 1.0 + 0.05 * jax.random.normal(k, shape, f32)
    sg_c = SG_WIDTH // SG_GROUPS
    return {
        "x_prompt": nrm(ks[0], (BATCH, SEQ, D_MODEL), 1.0),
        "x_sample": nrm(ks[1], (DEC_BATCH, DEC_SEQ, D_MODEL), 1.0),
        "p_prompt": nrm(ks[2], (DEPTH, BATCH, SEQ, PLE_DIM), 1.0),
        "p_sample": nrm(ks[3], (DEPTH, DEC_BATCH, DEC_SEQ, PLE_DIM), 1.0),
        "g_mix": gain(ks[4], (DEPTH, D_MODEL)),
        "w_in": nrm(ks[5], (DEPTH, D_MODEL, IN_COLS), D_MODEL ** -0.5),
        "q_norm": gain(ks[6], (DEPTH, HEAD_DIM)),
        "k_norm": gain(ks[7], (DEPTH, HEAD_DIM)),
        "sinks": nrm(ks[8], (DEPTH, N_HEADS), 1.0),
        "g_sg_v": gain(ks[9], (DEPTH, SG_WIDTH)),
        "w_s": nrm(ks[10], (DEPTH, SG_GROUPS, SG_CHUNK, SG_CHUNK), SG_CHUNK ** -0.5),
        "b_s": gain(ks[11], (DEPTH, SG_GROUPS, SG_CHUNK)),
        "w_branch": nrm(ks[12], (DEPTH, N_BRANCHES, SG_WIDTH, D_MODEL), SG_WIDTH ** -0.5),
        "w_o": nrm(ks[13], (DEPTH, D_MODEL, D_MODEL), D_MODEL ** -0.5),
        "g_ffn": gain(ks[14], (DEPTH, D_MODEL)),
        "w_router": nrm(ks[15], (DEPTH, D_MODEL, N_EXPERTS), D_MODEL ** -0.5),
        "b_router": nrm(ks[16], (DEPTH, N_EXPERTS), 0.01),
        "w_gu": nrm(ks[17], (DEPTH, N_EXPERTS, D_MODEL, 2 * D_FF), D_MODEL ** -0.5),
        "b_gu": nrm(ks[18], (DEPTH, N_EXPERTS, 2 * D_FF), 0.02),
        "w_down": nrm(ks[19], (DEPTH, N_EXPERTS, D_FF, D_MODEL), D_FF ** -0.5),
        "b_down": nrm(ks[20], (DEPTH, N_EXPERTS, D_MODEL), 0.02),
        "g_ple": gain(ks[21], (DEPTH, D_MODEL)),
        "w_ple_gate": nrm(ks[22], (DEPTH, D_MODEL, D_MODEL), D_MODEL ** -0.5),
        "w_ple_proj": nrm(ks[23], (DEPTH, PLE_DIM, D_MODEL), PLE_DIM ** -0.5),
    }


def reference(x_prompt, x_sample, p_prompt, p_sample, g_mix, w_in, q_norm, k_norm, sinks, g_sg_v, w_s, b_s,
              w_branch, w_o, g_ffn, w_router, b_router, w_gu, b_gu, w_down, b_down, g_ple, w_ple_gate, w_ple_proj):
    y_prompt = trunk(x_prompt, p_prompt, g_mix, w_in, q_norm, k_norm, sinks, g_sg_v, w_s, b_s, w_branch, w_o,
                     g_ffn, w_router, b_router, w_gu, b_gu, w_down, b_down, g_ple, w_ple_gate, w_ple_proj)
    y_sample = trunk(x_sample, p_sample, g_mix, w_in, q_norm, k_norm, sinks, g_sg_v, w_s, b_s, w_branch, w_o,
                     g_ffn, w_router, b_router, w_gu, b_gu, w_down, b_down, g_ple, w_ple_gate, w_ple_proj)
    return (y_prompt, y_sample)
```

```python
import functools

import jax
import jax.numpy as jnp
from jax import lax
from jax.experimental import pallas as pl
from jax.experimental.pallas import tpu as pltpu

F32 = jnp.float32
BF16 = jnp.bfloat16

D_MODEL = 2048
HEAD_DIM = 128
N_HEADS = 16
N_KV_HEADS = 4
GROUP = N_HEADS // N_KV_HEADS
WINDOW = 128
ROPE_THETA = 10000.0
SG_GROUPS = 16
SG_WIDTH = 2048
SG_CHUNK = 128
N_EXPERTS = 32
TOP_K = 4
D_FF = 2048
SWIGLU_LIMIT = 7.0
SWIGLU_ALPHA = 1.702
PLE_DIM = 256
EPS = 1e-6
NEG_INF = -1e30

Q_COLS = N_HEADS * HEAD_DIM
KV_COLS = N_KV_HEADS * HEAD_DIM
IN_COLS = Q_COLS + 2 * KV_COLS + 2 * SG_WIDTH + 2 * D_MODEL

COL_Q = 0
COL_U = Q_COLS
COL_SV = COL_U + SG_WIDTH
COL_GA = COL_SV + SG_WIDTH
COL_GS = COL_GA + D_MODEL
COL_K = COL_GS + D_MODEL
COL_V = COL_K + KV_COLS

VMEM_LIMIT = 56 * 1024 * 1024

IN_TM, IN_TN = 1024, 512
ATT_QB = 512
MIX_TM, MIX_TN = 512, 512
OUT_TM, OUT_TN = 1024, 512
RT_TM = 512
MOE_BM, MOE_FC = 512, 512
PLE_TM, PLE_TN = 1024, 512


def _params(sem):
    return pltpu.CompilerParams(dimension_semantics=sem, vmem_limit_bytes=VMEM_LIMIT)


def _in_proj_kernel(x_ref, g_ref, w_ref, qn_ref, kn_ref, cos_ref, sin_ref, o_ref, hn_ref):
    j = pl.program_id(1)

    @pl.when(j == 0)
    def _():
        x = x_ref[...]
        ms = jnp.mean(x * x, axis=-1, keepdims=True)
        hn_ref[...] = (x * lax.rsqrt(ms + EPS) * g_ref[...]).astype(BF16)

    acc = jnp.dot(hn_ref[...], w_ref[...], preferred_element_type=F32)

    jq = COL_U // IN_TN
    ju = COL_GA // IN_TN
    jg = COL_K // IN_TN
    jk = COL_V // IN_TN

    def qk_epilogue(gain):
        cos = cos_ref[...]
        sin = sin_ref[...]
        for h in range(IN_TN // HEAD_DIM):
            a = acc[:, h * HEAD_DIM:(h + 1) * HEAD_DIM]
            ms = jnp.mean(a * a, axis=-1, keepdims=True)
            an = a * lax.rsqrt(ms + EPS) * gain
            rot = pltpu.roll(an, HEAD_DIM // 2, axis=1)
            o_ref[:, h * HEAD_DIM:(h + 1) * HEAD_DIM] = (an * cos + rot * sin).astype(BF16)

    @pl.when(j < jq)
    def _():
        qk_epilogue(qn_ref[...])

    @pl.when(jnp.logical_and(j >= jq, j < ju))
    def _():
        o_ref[...] = jax.nn.gelu(acc).astype(BF16)

    @pl.when(jnp.logical_and(j >= ju, j < jg))
    def _():
        o_ref[...] = jax.nn.sigmoid(acc).astype(BF16)

    @pl.when(jnp.logical_and(j >= jg, j < jk))
    def _():
        qk_epilogue(kn_ref[...])

    @pl.when(j >= jk)
    def _():
        o_ref[...] = acc.astype(BF16)


def _in_proj(x, g_mix, w_in_r, q_norm, k_norm, cos_t, sin_t):
    t = x.shape[0]
    grid = (t // IN_TM, IN_COLS // IN_TN)
    return pl.pallas_call(
        _in_proj_kernel,
        grid=grid,
        in_specs=[
            pl.BlockSpec((IN_TM, D_MODEL), lambda i, j: (i, 0)),
            pl.BlockSpec((1, D_MODEL), lambda i, j: (0, 0)),
            pl.BlockSpec((D_MODEL, IN_TN), lambda i, j: (0, j)),
            pl.BlockSpec((1, HEAD_DIM), lambda i, j: (0, 0)),
            pl.BlockSpec((1, HEAD_DIM), lambda i, j: (0, 0)),
            pl.BlockSpec((IN_TM, HEAD_DIM), lambda i, j: (i, 0)),
            pl.BlockSpec((IN_TM, HEAD_DIM), lambda i, j: (i, 0)),
        ],
        out_specs=pl.BlockSpec((IN_TM, IN_TN), lambda i, j: (i, j)),
        out_shape=jax.ShapeDtypeStruct((t, IN_COLS), BF16),
        scratch_shapes=[pltpu.VMEM((IN_TM, D_MODEL), BF16)],
        compiler_params=_params(("parallel", "arbitrary")),
        name="in_proj",
    )(x, g_mix, w_in_r, q_norm, k_norm, cos_t, sin_t)


def _attn_kernel(first_ref, last_ref, sinks_ref, q_ref, kp_ref, km_ref, kn_ref,
                 vp_ref, vm_ref, vn_ref, o_ref):
    i = pl.program_id(0)
    kv = pl.program_id(1)
    nqb = ATT_QB // WINDOW
    kband = jnp.concatenate([kp_ref[...], km_ref[...], kn_ref[...]], axis=0)
    vband = jnp.concatenate([vp_ref[...], vm_ref[...], vn_ref[...]], axis=0)
    r = lax.broadcasted_iota(jnp.int32, (WINDOW, 3 * WINDOW), 0)
    c = lax.broadcasted_iota(jnp.int32, (WINDOW, 3 * WINDOW), 1)
    d = c - r
    band_ok = jnp.logical_and(d >= 0, d <= 2 * WINDOW)
    lo = jnp.where(first_ref[i] != 0, WINDOW, 0)
    hi = jnp.where(last_ref[i] != 0, 2 * WINDOW, 3 * WINDOW)
    scale = HEAD_DIM ** -0.5
    for b in range(nqb):
        ok = band_ok
        if b == 0:
            ok = jnp.logical_and(ok, c >= lo)
        if b == nqb - 1:
            ok = jnp.logical_and(ok, c < hi)
        kb = kband[b * WINDOW:(b + 3) * WINDOW]
        vb = vband[b * WINDOW:(b + 3) * WINDOW]
        for g in range(GROUP):
            qg = q_ref[b * WINDOW:(b + 1) * WINDOW, g * HEAD_DIM:(g + 1) * HEAD_DIM]
            s = lax.dot_general(qg, kb, (((1,), (1,)), ((), ())), preferred_element_type=F32)
            s = jnp.where(ok, s * scale, NEG_INF)
            sink = sinks_ref[kv * GROUP + g]
            m = jnp.maximum(jnp.max(s, axis=-1, keepdims=True), sink)
            p = jnp.exp(s - m)
            den = jnp.sum(p, axis=-1, keepdims=True) + jnp.exp(sink - m)
            o = jnp.dot(p.astype(BF16), vb, preferred_element_type=F32) / den
            o_ref[b * WINDOW:(b + 1) * WINDOW, g * HEAD_DIM:(g + 1) * HEAD_DIM] = o.astype(BF16)


def _attention(proj, sinks, first, last):
    t = proj.shape[0]
    nqb = ATT_QB // WINDOW
    nrb = t // WINDOW
    kcol = COL_K // HEAD_DIM
    vcol = COL_V // HEAD_DIM
    qspec = pl.BlockSpec((ATT_QB, GROUP * HEAD_DIM), lambda i, kv, *_: (i, kv))

    def band_specs(col):
        return [
            pl.BlockSpec((WINDOW, HEAD_DIM), lambda i, kv, *_: (jnp.maximum(i * nqb - 1, 0), col + kv)),
            pl.BlockSpec((ATT_QB, HEAD_DIM), lambda i, kv, *_: (i, col + kv)),
            pl.BlockSpec((WINDOW, HEAD_DIM), lambda i, kv, *_: (jnp.minimum((i + 1) * nqb, nrb - 1), col + kv)),
        ]

    grid_spec = pltpu.PrefetchScalarGridSpec(
        num_scalar_prefetch=3,
        grid=(t // ATT_QB, N_KV_HEADS),
        in_specs=[qspec] + band_specs(kcol) + band_specs(vcol),
        out_specs=pl.BlockSpec((ATT_QB, GROUP * HEAD_DIM), lambda i, kv, *_: (i, kv)),
    )
    return pl.pallas_call(
        _attn_kernel,
        grid_spec=grid_spec,
        out_shape=jax.ShapeDtypeStruct((t, Q_COLS), BF16),
        compiler_params=_params(("parallel", "arbitrary")),
        name="window_attn",
    )(first, last, sinks, proj, proj, proj, proj, proj, proj, proj)


def _mix_kernel(attn_ref, u_ref, sv_ref, ga_ref, gs_ref, wa_ref, wb_ref, ws_ref, bs_ref,
                gsv_ref, o_ref, vn_ref, sg_ref):
    j = pl.program_id(1)

    @pl.when(j == 0)
    def _():
        sv = sv_ref[...].astype(F32)
        ms = jnp.mean(sv * sv, axis=-1, keepdims=True)
        vn_ref[...] = (sv * lax.rsqrt(ms + EPS) * gsv_ref[...]).astype(BF16)

        def chunk(c, carry):
            rows = pl.ds(pl.multiple_of(c * SG_CHUNK, SG_CHUNK), SG_CHUNK)
            for g in range(SG_GROUPS):
                cols = slice(g * SG_CHUNK, (g + 1) * SG_CHUNK)
                mixed = jnp.dot(ws_ref[g], vn_ref[rows, cols], preferred_element_type=F32) + bs_ref[g]
                sg_ref[rows, cols] = (u_ref[rows, cols].astype(F32) * mixed).astype(BF16)
            return carry

        lax.fori_loop(0, MIX_TM // SG_CHUNK, chunk, 0)

    pa = jnp.dot(attn_ref[...], wa_ref[...], preferred_element_type=F32)
    ps = jnp.dot(sg_ref[...], wb_ref[...], preferred_element_type=F32)
    merged = ga_ref[...].astype(F32) * pa + gs_ref[...].astype(F32) * ps
    o_ref[...] = merged.astype(BF16)


def _mix(attn, proj, wa, wb, ws, bs_b, gsv):
    t = attn.shape[0]
    grid = (t // MIX_TM, D_MODEL // MIX_TN)
    ga0 = COL_GA // MIX_TN
    gs0 = COL_GS // MIX_TN
    return pl.pallas_call(
        _mix_kernel,
        grid=grid,
        in_specs=[
            pl.BlockSpec((MIX_TM, Q_COLS), lambda i, j: (i, 0)),
            pl.BlockSpec((MIX_TM, SG_WIDTH), lambda i, j: (i, COL_U // SG_WIDTH)),
            pl.BlockSpec((MIX_TM, SG_WIDTH), lambda i, j: (i, COL_SV // SG_WIDTH)),
            pl.BlockSpec((MIX_TM, MIX_TN), lambda i, j: (i, ga0 + j)),
            pl.BlockSpec((MIX_TM, MIX_TN), lambda i, j: (i, gs0 + j)),
            pl.BlockSpec((Q_COLS, MIX_TN), lambda i, j: (0, j)),
            pl.BlockSpec((SG_WIDTH, MIX_TN), lambda i, j: (0, j)),
            pl.BlockSpec((SG_GROUPS, SG_CHUNK, SG_CHUNK), lambda i, j: (0, 0, 0)),
            pl.BlockSpec((SG_GROUPS, SG_CHUNK, SG_CHUNK), lambda i, j: (0, 0, 0)),
            pl.BlockSpec((1, SG_WIDTH), lambda i, j: (0, 0)),
        ],
        out_specs=pl.BlockSpec((MIX_TM, MIX_TN), lambda i, j: (i, j)),
        out_shape=jax.ShapeDtypeStruct((t, D_MODEL), BF16),
        scratch_shapes=[pltpu.VMEM((MIX_TM, SG_WIDTH), BF16), pltpu.VMEM((MIX_TM, SG_WIDTH), BF16)],
        compiler_params=_params(("parallel", "arbitrary")),
        name="branch_mix",
    )(attn, proj, proj, proj, proj, wa, wb, ws, bs_b, gsv)


def _out_proj_kernel(m_ref, w_ref, x_ref, o_ref):
    o_ref[...] = x_ref[...] + jnp.dot(m_ref[...], w_ref[...], preferred_element_type=F32)


def _out_proj(merged, w_o, x):
    t = x.shape[0]
    return pl.pallas_call(
        _out_proj_kernel,
        grid=(t // OUT_TM, D_MODEL // OUT_TN),
        in_specs=[
            pl.BlockSpec((OUT_TM, D_MODEL), lambda i, j: (i, 0)),
            pl.BlockSpec((D_MODEL, OUT_TN), lambda i, j: (0, j)),
            pl.BlockSpec((OUT_TM, OUT_TN), lambda i, j: (i, j)),
        ],
        out_specs=pl.BlockSpec((OUT_TM, OUT_TN), lambda i, j: (i, j)),
        out_shape=jax.ShapeDtypeStruct((t, D_MODEL), F32),
        compiler_params=_params(("parallel", "arbitrary")),
        name="out_proj",
    )(merged, w_o, x)


def _split_bf16(a):
    hi = a.astype(BF16)
    lo = (a - hi.astype(F32)).astype(BF16)
    return hi, lo


def _router_kernel(x_ref, g_ref, wr_ref, br_ref, h_ref, idx_ref, wgt_ref, rank_ref, cnt_ref, carry_ref):
    step = pl.program_id(0)

    @pl.when(step == 0)
    def _():
        carry_ref[...] = jnp.zeros_like(carry_ref)

    x = x_ref[...]
    ms = jnp.mean(x * x, axis=-1, keepdims=True)
    h = x * lax.rsqrt(ms + EPS) * g_ref[...]
    h_ref[...] = h.astype(BF16)

    h_hi, h_lo = _split_bf16(h)
    w_hi, w_lo = _split_bf16(wr_ref[...])
    dn = (((1,), (1,)), ((), ()))
    logits = (lax.dot_general(w_hi, h_hi, dn, preferred_element_type=F32)
              + lax.dot_general(w_hi, h_lo, dn, preferred_element_type=F32)
              + lax.dot_general(w_lo, h_hi, dn, preferred_element_type=F32))
    logits = logits + br_ref[...]

    e_iota = lax.broadcasted_iota(jnp.int32, (N_EXPERTS, RT_TM), 0)
    vals = logits
    tops, idxs = [], []
    onehot = jnp.zeros((N_EXPERTS, RT_TM), F32)
    for _ in range(TOP_K):
        m = jnp.max(vals, axis=0, keepdims=True)
        idx = jnp.min(jnp.where(vals == m, e_iota, N_EXPERTS), axis=0, keepdims=True)
        sel = e_iota == idx
        onehot = onehot + sel.astype(F32)
        vals = jnp.where(sel, -jnp.inf, vals)
        tops.append(m)
        idxs.append(idx)

    exps = [jnp.exp(v - tops[0]) for v in tops]
    den = exps[0] + exps[1] + exps[2] + exps[3]

    tr = lax.broadcasted_iota(jnp.int32, (RT_TM, RT_TM), 0)
    tc = lax.broadcasted_iota(jnp.int32, (RT_TM, RT_TM), 1)
    upper = jnp.where(tr < tc, 1.0, 0.0).astype(BF16)
    before = jnp.dot(onehot.astype(BF16), upper, preferred_element_type=F32) + carry_ref[:, 0:1]

    for k in range(TOP_K):
        sel = e_iota == idxs[k]
        rank = jnp.sum(jnp.where(sel, before, 0.0), axis=0, keepdims=True)
        idx_ref[k:k + 1, :] = idxs[k]
        wgt_ref[k:k + 1, :] = exps[k] / den
        rank_ref[k:k + 1, :] = rank.astype(jnp.int32)

    carry_ref[...] = carry_ref[...] + jnp.sum(onehot, axis=1, keepdims=True)
    cnt_ref[...] = carry_ref[...]


def _router(x1, g_ffn, w_router_t, b_router):
    t = x1.shape[0]
    return pl.pallas_call(
        _router_kernel,
        grid=(t // RT_TM,),
        in_specs=[
            pl.BlockSpec((RT_TM, D_MODEL), lambda i: (i, 0)),
            pl.BlockSpec((1, D_MODEL), lambda i: (0, 0)),
            pl.BlockSpec((N_EXPERTS, D_MODEL), lambda i: (0, 0)),
            pl.BlockSpec((N_EXPERTS, 1), lambda i: (0, 0)),
        ],
        out_specs=[
            pl.BlockSpec((RT_TM, D_MODEL), lambda i: (i, 0)),
            pl.BlockSpec((TOP_K, RT_TM), lambda i: (0, i)),
            pl.BlockSpec((TOP_K, RT_TM), lambda i: (0, i)),
            pl.BlockSpec((TOP_K, RT_TM), lambda i: (0, i)),
            pl.BlockSpec((N_EXPERTS, 128), lambda i: (0, 0)),
        ],
        out_shape=[
            jax.ShapeDtypeStruct((t, D_MODEL), BF16),
            jax.ShapeDtypeStruct((TOP_K, t), jnp.int32),
            jax.ShapeDtypeStruct((TOP_K, t), F32),
            jax.ShapeDtypeStruct((TOP_K, t), jnp.int32),
            jax.ShapeDtypeStruct((N_EXPERTS, 128), F32),
        ],
        scratch_shapes=[pltpu.VMEM((N_EXPERTS, 128), F32)],
        compiler_params=_params(("arbitrary",)),
        name="router",
    )(x1, g_ffn, w_router_t, b_router)


def _moe_kernel(be_ref, na_ref, x_ref, wg_ref, wu_ref, bg_ref, bu_ref, wd_ref, bd_ref, o_ref, acc_ref):
    b = pl.program_id(0)
    j = pl.program_id(1)
    nj = pl.num_programs(1)
    active = b < na_ref[0]

    @pl.when(jnp.logical_and(active, j == 0))
    def _():
        acc_ref[...] = jnp.zeros_like(acc_ref)

    @pl.when(active)
    def _():
        x = x_ref[...]
        gate = jnp.dot(x, wg_ref[...], preferred_element_type=F32) + bg_ref[...]
        up = jnp.dot(x, wu_ref[...], preferred_element_type=F32) + bu_ref[...]
        gate = jnp.minimum(gate, SWIGLU_LIMIT)
        up = jnp.clip(up, -SWIGLU_LIMIT, SWIGLU_LIMIT)
        act = (up + 1.0) * gate * jax.nn.sigmoid(SWIGLU_ALPHA * gate)
        acc_ref[...] += jnp.dot(act.astype(BF16), wd_ref[...], preferred_element_type=F32)

    @pl.when(jnp.logical_and(active, j == nj - 1))
    def _():
        o_ref[...] = acc_ref[...] + bd_ref[...]

    @pl.when(jnp.logical_and(jnp.logical_not(active), j == nj - 1))
    def _():
        o_ref[...] = jnp.zeros_like(o_ref)


def _moe(block_e, n_active, xs, w_gu, b_gu, w_down, b_down):
    n_pad = xs.shape[0]
    nfc = D_FF // MOE_FC
    grid_spec = pltpu.PrefetchScalarGridSpec(
        num_scalar_prefetch=2,
        grid=(n_pad // MOE_BM, nfc),
        in_specs=[
            pl.BlockSpec((MOE_BM, D_MODEL), lambda b, j, be, na: (b, 0)),
            pl.BlockSpec((None, D_MODEL, MOE_FC), lambda b, j, be, na: (be[b], 0, j)),
            pl.BlockSpec((None, D_MODEL, MOE_FC), lambda b, j, be, na: (be[b], 0, nfc + j)),
            pl.BlockSpec((None, 1, MOE_FC), lambda b, j, be, na: (be[b], 0, j)),
            pl.BlockSpec((None, 1, MOE_FC), lambda b, j, be, na: (be[b], 0, nfc + j)),
            pl.BlockSpec((None, MOE_FC, D_MODEL), lambda b, j, be, na: (be[b], j, 0)),
            pl.BlockSpec((None, 1, D_MODEL), lambda b, j, be, na: (be[b], 0, 0)),
        ],
        out_specs=pl.BlockSpec((MOE_BM, D_MODEL), lambda b, j, be, na: (b, 0)),
        scratch_shapes=[pltpu.VMEM((MOE_BM, D_MODEL), F32)],
    )
    return pl.pallas_call(
        _moe_kernel,
        grid_spec=grid_spec,
        out_shape=jax.ShapeDtypeStruct((n_pad, D_MODEL), F32),
        compiler_params=_params(("arbitrary", "arbitrary")),
        name="moe_experts",
    )(block_e, n_active, xs, w_gu, w_gu, b_gu, b_gu, w_down, b_down)


def _ple_kernel(xf_ref, xb_ref, g_ref, p_ref, wg_ref, wp_ref, o_ref, hn_ref):
    j = pl.program_id(1)

    @pl.when(j == 0)
    def _():
        x = xf_ref[...]
        ms = jnp.mean(x * x, axis=-1, keepdims=True)
        hn_ref[...] = (x * lax.rsqrt(ms + EPS) * g_ref[...]).astype(BF16)

    gate = jax.nn.sigmoid(jnp.dot(hn_ref[...], wg_ref[...], preferred_element_type=F32))
    pp = jnp.dot(p_ref[...], wp_ref[...], preferred_element_type=F32)
    o_ref[...] = xb_ref[...] + gate * pp


def _ple(x2, g_ple, p, w_gate, w_proj):
    t = x2.shape[0]
    return pl.pallas_call(
        _ple_kernel,
        grid=(t // PLE_TM, D_MODEL // PLE_TN),
        in_specs=[
            pl.BlockSpec((PLE_TM, D_MODEL), lambda i, j: (i, 0)),
            pl.BlockSpec((PLE_TM, PLE_TN), lambda i, j: (i, j)),
            pl.BlockSpec((1, D_MODEL), lambda i, j: (0, 0)),
            pl.BlockSpec((PLE_TM, PLE_DIM), lambda i, j: (i, 0)),
            pl.BlockSpec((D_MODEL, PLE_TN), lambda i, j: (0, j)),
            pl.BlockSpec((PLE_DIM, PLE_TN), lambda i, j: (0, j)),
        ],
        out_specs=pl.BlockSpec((PLE_TM, PLE_TN), lambda i, j: (i, j)),
        out_shape=jax.ShapeDtypeStruct((t, D_MODEL), F32),
        scratch_shapes=[pltpu.VMEM((PLE_TM, D_MODEL), BF16)],
        compiler_params=_params(("parallel", "arbitrary")),
        name="ple",
    )(x2, x2, g_ple, p, w_gate, w_proj)


def _rope_tables(seq_lens):
    half = HEAD_DIM // 2
    inv_freq = ROPE_THETA ** (-jnp.arange(half, dtype=F32) / half)
    cos_parts, sin_parts = [], []
    for s in seq_lens:
        ang = jnp.arange(s, dtype=F32)[:, None] * inv_freq[None, :]
        cos, sin = jnp.cos(ang), jnp.sin(ang)
        cos_parts.append(jnp.concatenate([cos, cos], axis=-1))
        sin_parts.append(jnp.concatenate([-sin, sin], axis=-1))
    return jnp.concatenate(cos_parts, axis=0), jnp.concatenate(sin_parts, axis=0)


def _seq_edge_flags(seq_lens):
    first, last = [], []
    for s in seq_lens:
        nb = s // ATT_QB
        first += [1] + [0] * (nb - 1)
        last += [0] * (nb - 1) + [1]
    return jnp.array(first, jnp.int32), jnp.array(last, jnp.int32)


def _layer(x, p, seq_lens, g_mix, w_in, q_norm, k_norm, sinks, g_sg_v, w_s, b_s, w_branch, w_o,
           g_ffn, w_router, b_router, w_gu, b_gu, w_down, b_down, g_ple, w_ple_gate, w_ple_proj):
    t = x.shape[0]
    q_end = Q_COLS
    kv_end = Q_COLS + 2 * KV_COLS
    w_in_r = jnp.concatenate([w_in[:, :q_end], w_in[:, kv_end:], w_in[:, q_end:kv_end]], axis=1).astype(BF16)
    cos_t, sin_t = _rope_tables(seq_lens)
    first, last = _seq_edge_flags(seq_lens)

    proj = _in_proj(x, g_mix[None, :], w_in_r, q_norm[None, :], k_norm[None, :], cos_t, sin_t)
    attn = _attention(proj, sinks, first, last)
    bs_b = jnp.broadcast_to(b_s[:, :, None], (SG_GROUPS, SG_CHUNK, SG_CHUNK))
    merged = _mix(attn, proj, w_branch[0].astype(BF16), w_branch[1].astype(BF16), w_s.astype(BF16), bs_b,
                  g_sg_v[None, :])
    x1 = _out_proj(merged, w_o.astype(BF16), x)

    h2, idx, wgt, rank, cnt = _router(x1, g_ffn[None, :], w_router.T, b_router[:, None])

    counts = cnt[:, 0].astype(jnp.int32)
    padded = (counts + MOE_BM - 1) // MOE_BM * MOE_BM
    padded_end = jnp.cumsum(padded)
    padded_start = padded_end - padded
    dest = padded_start[idx] + rank
    n_blocks = (t * TOP_K) // MOE_BM + N_EXPERTS
    block_e = jnp.minimum(
        jnp.searchsorted(padded_end, jnp.arange(n_blocks, dtype=jnp.int32) * MOE_BM, side='right'),
        N_EXPERTS - 1).astype(jnp.int32)
    n_active = (padded_end[-1:] // MOE_BM).astype(jnp.int32)

    xs = jnp.zeros((n_blocks * MOE_BM, D_MODEL), BF16)
    for k in range(TOP_K):
        xs = xs.at[dest[k]].set(h2)
    ys = _moe(block_e, n_active, xs, w_gu.astype(BF16), b_gu[:, None, :], w_down.astype(BF16),
              b_down[:, None, :])
    x2 = x1
    for k in range(TOP_K):
        x2 = x2 + ys[dest[k]] * wgt[k][:, None]

    return _ple(x2, g_ple[None, :], p.astype(BF16), w_ple_gate.astype(BF16), w_ple_proj.astype(BF16))


def kernel(x_prompt, x_sample, p_prompt, p_sample, g_mix, w_in, q_norm, k_norm, sinks, g_sg_v, w_s, b_s,
           w_branch, w_o, g_ffn, w_router, b_router, w_gu, b_gu, w_down, b_down, g_ple, w_ple_gate,
           w_ple_proj):
    depth = g_mix.shape[0]
    bp, sp, d = x_prompt.shape
    bs, ss, _ = x_sample.shape
    seq_lens = [sp] * bp + [ss] * bs
    n_p = bp * sp
    x = jnp.concatenate([x_prompt.reshape(n_p, d), x_sample.reshape(bs * ss, d)], axis=0)
    for l in range(depth):
        p = jnp.concatenate([p_prompt[l].reshape(n_p, PLE_DIM), p_sample[l].reshape(bs * ss, PLE_DIM)], axis=0)
        x = _layer(x, p, seq_lens, g_mix[l], w_in[l], q_norm[l], k_norm[l], sinks[l], g_sg_v[l], w_s[l], b_s[l],
                   w_branch[l], w_o[l], g_ffn[l], w_router[l], b_router[l], w_gu[l], b_gu[l], w_down[l],
                   b_down[l], g_ple[l], w_ple_gate[l], w_ple_proj[l])
    return x[:n_p].reshape(bp, sp, d), x[n_p:].reshape(bs, ss, d)
```

```python
import functools

import jax
import jax.numpy as jnp
from jax import lax
from jax.experimental import pallas as pl
from jax.experimental.pallas import tpu as pltpu

F32 = jnp.float32
BF16 = jnp.bfloat16

D_MODEL = 2048
HEAD_DIM = 128
N_HEADS = 16
N_KV_HEADS = 4
GROUP = N_HEADS // N_KV_HEADS
WINDOW = 128
ROPE_THETA = 10000.0
SG_GROUPS = 16
SG_WIDTH = 2048
SG_CHUNK = 128
N_EXPERTS = 32
TOP_K = 4
D_FF = 2048
SWIGLU_LIMIT = 7.0
SWIGLU_ALPHA = 1.702
PLE_DIM = 256
EPS = 1e-6
NEG_INF = -1e30

Q_COLS = N_HEADS * HEAD_DIM
KV_COLS = N_KV_HEADS * HEAD_DIM
IN_COLS = Q_COLS + 2 * KV_COLS + 2 * SG_WIDTH + 2 * D_MODEL

COL_Q = 0
COL_U = Q_COLS
COL_SV = COL_U + SG_WIDTH
COL_GA = COL_SV + SG_WIDTH
COL_GS = COL_GA + D_MODEL
COL_K = COL_GS + D_MODEL
COL_V = COL_K + KV_COLS

VMEM_LIMIT = 56 * 1024 * 1024

IN_TM, IN_TN = 1024, 512
ATT_QB = 512
MIX_TM, MIX_TN = 512, 512
OUT_TM, OUT_TN = 1024, 512
RT_TM = 512
MOE_BM, MOE_FC = 512, 512
DSP_TB = 256
CMB_TB = 256
PLE_TM, PLE_TN = 1024, 512


def _params(sem):
    return pltpu.CompilerParams(dimension_semantics=sem, vmem_limit_bytes=VMEM_LIMIT)


def _in_proj_kernel(x_ref, g_ref, w_ref, qn_ref, kn_ref, cos_ref, sin_ref, o_ref, hn_ref):
    j = pl.program_id(1)

    @pl.when(j == 0)
    def _():
        x = x_ref[...]
        ms = jnp.mean(x * x, axis=-1, keepdims=True)
        hn_ref[...] = (x * lax.rsqrt(ms + EPS) * g_ref[...]).astype(BF16)

    acc = jnp.dot(hn_ref[...], w_ref[...], preferred_element_type=F32)

    jq = COL_U // IN_TN
    ju = COL_GA // IN_TN
    jg = COL_K // IN_TN
    jk = COL_V // IN_TN

    def qk_epilogue(gain):
        cos = cos_ref[...]
        sin = sin_ref[...]
        for h in range(IN_TN // HEAD_DIM):
            a = acc[:, h * HEAD_DIM:(h + 1) * HEAD_DIM]
            ms = jnp.mean(a * a, axis=-1, keepdims=True)
            an = a * lax.rsqrt(ms + EPS) * gain
            rot = pltpu.roll(an, HEAD_DIM // 2, axis=1)
            o_ref[:, h * HEAD_DIM:(h + 1) * HEAD_DIM] = (an * cos + rot * sin).astype(BF16)

    @pl.when(j < jq)
    def _():
        qk_epilogue(qn_ref[...])

    @pl.when(jnp.logical_and(j >= jq, j < ju))
    def _():
        o_ref[...] = jax.nn.gelu(acc).astype(BF16)

    @pl.when(jnp.logical_and(j >= ju, j < jg))
    def _():
        o_ref[...] = jax.nn.sigmoid(acc).astype(BF16)

    @pl.when(jnp.logical_and(j >= jg, j < jk))
    def _():
        qk_epilogue(kn_ref[...])

    @pl.when(j >= jk)
    def _():
        o_ref[...] = acc.astype(BF16)


def _in_proj(x, g_mix, w_in_r, q_norm, k_norm, cos_t, sin_t):
    t = x.shape[0]
    grid = (t // IN_TM, IN_COLS // IN_TN)
    return pl.pallas_call(
        _in_proj_kernel,
        grid=grid,
        in_specs=[
            pl.BlockSpec((IN_TM, D_MODEL), lambda i, j: (i, 0)),
            pl.BlockSpec((1, D_MODEL), lambda i, j: (0, 0)),
            pl.BlockSpec((D_MODEL, IN_TN), lambda i, j: (0, j)),
            pl.BlockSpec((1, HEAD_DIM), lambda i, j: (0, 0)),
            pl.BlockSpec((1, HEAD_DIM), lambda i, j: (0, 0)),
            pl.BlockSpec((IN_TM, HEAD_DIM), lambda i, j: (i, 0)),
            pl.BlockSpec((IN_TM, HEAD_DIM), lambda i, j: (i, 0)),
        ],
        out_specs=pl.BlockSpec((IN_TM, IN_TN), lambda i, j: (i, j)),
        out_shape=jax.ShapeDtypeStruct((t, IN_COLS), BF16),
        scratch_shapes=[pltpu.VMEM((IN_TM, D_MODEL), BF16)],
        compiler_params=_params(("parallel", "arbitrary")),
        name="in_proj",
    )(x, g_mix, w_in_r, q_norm, k_norm, cos_t, sin_t)


def _attn_kernel(first_ref, last_ref, sinks_ref, q_ref, kp_ref, km_ref, kn_ref,
                 vp_ref, vm_ref, vn_ref, o_ref):
    i = pl.program_id(0)
    kv = pl.program_id(1)
    nqb = ATT_QB // WINDOW
    kband = jnp.concatenate([kp_ref[...], km_ref[...], kn_ref[...]], axis=0)
    vband = jnp.concatenate([vp_ref[...], vm_ref[...], vn_ref[...]], axis=0)
    r = lax.broadcasted_iota(jnp.int32, (WINDOW, 3 * WINDOW), 0)
    c = lax.broadcasted_iota(jnp.int32, (WINDOW, 3 * WINDOW), 1)
    d = c - r
    band_ok = jnp.logical_and(d >= 0, d <= 2 * WINDOW)
    lo = jnp.where(first_ref[i] != 0, WINDOW, 0)
    hi = jnp.where(last_ref[i] != 0, 2 * WINDOW, 3 * WINDOW)
    scale = HEAD_DIM ** -0.5
    for b in range(nqb):
        ok = band_ok
        if b == 0:
            ok = jnp.logical_and(ok, c >= lo)
        if b == nqb - 1:
            ok = jnp.logical_and(ok, c < hi)
        kb = kband[b * WINDOW:(b + 3) * WINDOW]
        vb = vband[b * WINDOW:(b + 3) * WINDOW]
        for g in range(GROUP):
            qg = q_ref[b * WINDOW:(b + 1) * WINDOW, g * HEAD_DIM:(g + 1) * HEAD_DIM]
            s = lax.dot_general(qg, kb, (((1,), (1,)), ((), ())), preferred_element_type=F32)
            s = jnp.where(ok, s * scale, NEG_INF)
            sink = sinks_ref[kv * GROUP + g]
            m = jnp.maximum(jnp.max(s, axis=-1, keepdims=True), sink)
            p = jnp.exp(s - m)
            den = jnp.sum(p, axis=-1, keepdims=True) + jnp.exp(sink - m)
            o = jnp.dot(p.astype(BF16), vb, preferred_element_type=F32) / den
            o_ref[b * WINDOW:(b + 1) * WINDOW, g * HEAD_DIM:(g + 1) * HEAD_DIM] = o.astype(BF16)


def _attention(proj, sinks, first, last):
    t = proj.shape[0]
    nqb = ATT_QB // WINDOW
    nrb = t // WINDOW
    kcol = COL_K // HEAD_DIM
    vcol = COL_V // HEAD_DIM
    qspec = pl.BlockSpec((ATT_QB, GROUP * HEAD_DIM), lambda i, kv, *_: (i, kv))

    def band_specs(col):
        return [
            pl.BlockSpec((WINDOW, HEAD_DIM), lambda i, kv, *_: (jnp.maximum(i * nqb - 1, 0), col + kv)),
            pl.BlockSpec((ATT_QB, HEAD_DIM), lambda i, kv, *_: (i, col + kv)),
            pl.BlockSpec((WINDOW, HEAD_DIM), lambda i, kv, *_: (jnp.minimum((i + 1) * nqb, nrb - 1), col + kv)),
        ]

    grid_spec = pltpu.PrefetchScalarGridSpec(
        num_scalar_prefetch=3,
        grid=(t // ATT_QB, N_KV_HEADS),
        in_specs=[qspec] + band_specs(kcol) + band_specs(vcol),
        out_specs=pl.BlockSpec((ATT_QB, GROUP * HEAD_DIM), lambda i, kv, *_: (i, kv)),
    )
    return pl.pallas_call(
        _attn_kernel,
        grid_spec=grid_spec,
        out_shape=jax.ShapeDtypeStruct((t, Q_COLS), BF16),
        compiler_params=_params(("parallel", "arbitrary")),
        name="window_attn",
    )(first, last, sinks, proj, proj, proj, proj, proj, proj, proj)


def _mix_kernel(attn_ref, u_ref, sv_ref, ga_ref, gs_ref, wa_ref, wb_ref, ws_ref, bs_ref,
                gsv_ref, o_ref, vn_ref, sg_ref):
    j = pl.program_id(1)

    @pl.when(j == 0)
    def _():
        sv = sv_ref[...].astype(F32)
        ms = jnp.mean(sv * sv, axis=-1, keepdims=True)
        vn_ref[...] = (sv * lax.rsqrt(ms + EPS) * gsv_ref[...]).astype(BF16)

        def chunk(c, carry):
            rows = pl.ds(pl.multiple_of(c * SG_CHUNK, SG_CHUNK), SG_CHUNK)
            for g in range(SG_GROUPS):
                cols = slice(g * SG_CHUNK, (g + 1) * SG_CHUNK)
                mixed = jnp.dot(ws_ref[g], vn_ref[rows, cols], preferred_element_type=F32) + bs_ref[g]
                sg_ref[rows, cols] = (u_ref[rows, cols].astype(F32) * mixed).astype(BF16)
            return carry

        lax.fori_loop(0, MIX_TM // SG_CHUNK, chunk, 0)

    pa = jnp.dot(attn_ref[...], wa_ref[...], preferred_element_type=F32)
    ps = jnp.dot(sg_ref[...], wb_ref[...], preferred_element_type=F32)
    merged = ga_ref[...].astype(F32) * pa + gs_ref[...].astype(F32) * ps
    o_ref[...] = merged.astype(BF16)


def _mix(attn, proj, wa, wb, ws, bs_b, gsv):
    t = attn.shape[0]
    grid = (t // MIX_TM, D_MODEL // MIX_TN)
    ga0 = COL_GA // MIX_TN
    gs0 = COL_GS // MIX_TN
    return pl.pallas_call(
        _mix_kernel,
        grid=grid,
        in_specs=[
            pl.BlockSpec((MIX_TM, Q_COLS), lambda i, j: (i, 0)),
            pl.BlockSpec((MIX_TM, SG_WIDTH), lambda i, j: (i, COL_U // SG_WIDTH)),
            pl.BlockSpec((MIX_TM, SG_WIDTH), lambda i, j: (i, COL_SV // SG_WIDTH)),
            pl.BlockSpec((MIX_TM, MIX_TN), lambda i, j: (i, ga0 + j)),
            pl.BlockSpec((MIX_TM, MIX_TN), lambda i, j: (i, gs0 + j)),
            pl.BlockSpec((Q_COLS, MIX_TN), lambda i, j: (0, j)),
            pl.BlockSpec((SG_WIDTH, MIX_TN), lambda i, j: (0, j)),
            pl.BlockSpec((SG_GROUPS, SG_CHUNK, SG_CHUNK), lambda i, j: (0, 0, 0)),
            pl.BlockSpec((SG_GROUPS, SG_CHUNK, SG_CHUNK), lambda i, j: (0, 0, 0)),
            pl.BlockSpec((1, SG_WIDTH), lambda i, j: (0, 0)),
        ],
        out_specs=pl.BlockSpec((MIX_TM, MIX_TN), lambda i, j: (i, j)),
        out_shape=jax.ShapeDtypeStruct((t, D_MODEL), BF16),
        scratch_shapes=[pltpu.VMEM((MIX_TM, SG_WIDTH), BF16), pltpu.VMEM((MIX_TM, SG_WIDTH), BF16)],
        compiler_params=_params(("parallel", "arbitrary")),
        name="branch_mix",
    )(attn, proj, proj, proj, proj, wa, wb, ws, bs_b, gsv)


def _out_proj_kernel(m_ref, w_ref, x_ref, o_ref):
    o_ref[...] = x_ref[...] + jnp.dot(m_ref[...], w_ref[...], preferred_element_type=F32)


def _out_proj(merged, w_o, x):
    t = x.shape[0]
    return pl.pallas_call(
        _out_proj_kernel,
        grid=(t // OUT_TM, D_MODEL // OUT_TN),
        in_specs=[
            pl.BlockSpec((OUT_TM, D_MODEL), lambda i, j: (i, 0)),
            pl.BlockSpec((D_MODEL, OUT_TN), lambda i, j: (0, j)),
            pl.BlockSpec((OUT_TM, OUT_TN), lambda i, j: (i, j)),
        ],
        out_specs=pl.BlockSpec((OUT_TM, OUT_TN), lambda i, j: (i, j)),
        out_shape=jax.ShapeDtypeStruct((t, D_MODEL), F32),
        compiler_params=_params(("parallel", "arbitrary")),
        name="out_proj",
    )(merged, w_o, x)


def _split_bf16(a):
    hi = a.astype(BF16)
    lo = (a - hi.astype(F32)).astype(BF16)
    return hi, lo


def _router_kernel(x_ref, g_ref, wr_ref, br_ref, idx_ref, wgt_ref, rank_ref, cnt_ref, carry_ref):
    step = pl.program_id(0)

    @pl.when(step == 0)
    def _():
        carry_ref[...] = jnp.zeros_like(carry_ref)

    x = x_ref[...]
    ms = jnp.mean(x * x, axis=-1, keepdims=True)
    h = x * lax.rsqrt(ms + EPS) * g_ref[...]

    h_hi, h_lo = _split_bf16(h)
    w_hi, w_lo = _split_bf16(wr_ref[...])
    dn = (((1,), (1,)), ((), ()))
    logits = (lax.dot_general(w_hi, h_hi, dn, preferred_element_type=F32)
              + lax.dot_general(w_hi, h_lo, dn, preferred_element_type=F32)
              + lax.dot_general(w_lo, h_hi, dn, preferred_element_type=F32))
    logits = logits + br_ref[...]

    e_iota = lax.broadcasted_iota(jnp.int32, (N_EXPERTS, RT_TM), 0)
    vals = logits
    tops, idxs = [], []
    onehot = jnp.zeros((N_EXPERTS, RT_TM), F32)
    for _ in range(TOP_K):
        m = jnp.max(vals, axis=0, keepdims=True)
        idx = jnp.min(jnp.where(vals == m, e_iota, N_EXPERTS), axis=0, keepdims=True)
        sel = e_iota == idx
        onehot = onehot + sel.astype(F32)
        vals = jnp.where(sel, -jnp.inf, vals)
        tops.append(m)
        idxs.append(idx)

    exps = [jnp.exp(v - tops[0]) for v in tops]
    den = exps[0] + exps[1] + exps[2] + exps[3]

    tr = lax.broadcasted_iota(jnp.int32, (RT_TM, RT_TM), 0)
    tc = lax.broadcasted_iota(jnp.int32, (RT_TM, RT_TM), 1)
    upper = jnp.where(tr < tc, 1.0, 0.0).astype(BF16)
    before = jnp.dot(onehot.astype(BF16), upper, preferred_element_type=F32) + carry_ref[:, 0:1]

    for k in range(TOP_K):
        sel = e_iota == idxs[k]
        rank = jnp.sum(jnp.where(sel, before, 0.0), axis=0, keepdims=True)
        idx_ref[k:k + 1, :] = idxs[k]
        wgt_ref[k:k + 1, :] = exps[k] / den
        rank_ref[k:k + 1, :] = rank.astype(jnp.int32)

    carry_ref[...] = carry_ref[...] + jnp.sum(onehot, axis=1, keepdims=True)
    cnt_ref[...] = carry_ref[...]


def _router(x1, g_ffn, w_router_t, b_router):
    t = x1.shape[0]
    return pl.pallas_call(
        _router_kernel,
        grid=(t // RT_TM,),
        in_specs=[
            pl.BlockSpec((RT_TM, D_MODEL), lambda i: (i, 0)),
            pl.BlockSpec((1, D_MODEL), lambda i: (0, 0)),
            pl.BlockSpec((N_EXPERTS, D_MODEL), lambda i: (0, 0)),
            pl.BlockSpec((N_EXPERTS, 1), lambda i: (0, 0)),
        ],
        out_specs=[
            pl.BlockSpec((TOP_K, RT_TM), lambda i: (0, i)),
            pl.BlockSpec((TOP_K, RT_TM), lambda i: (0, i)),
            pl.BlockSpec((TOP_K, RT_TM), lambda i: (0, i)),
            pl.BlockSpec((N_EXPERTS, 128), lambda i: (0, 0)),
        ],
        out_shape=[
            jax.ShapeDtypeStruct((TOP_K, t), jnp.int32),
            jax.ShapeDtypeStruct((TOP_K, t), F32),
            jax.ShapeDtypeStruct((TOP_K, t), jnp.int32),
            jax.ShapeDtypeStruct((N_EXPERTS, 128), F32),
        ],
        scratch_shapes=[pltpu.VMEM((N_EXPERTS, 128), F32)],
        compiler_params=_params(("arbitrary",)),
        name="router",
    )(x1, g_ffn, w_router_t, b_router)


def _row_block_wait(src, dst, sem, copies):
    for _ in range(copies):
        pltpu.make_async_copy(src, dst, sem).wait()


def _dispatch_kernel(padstart_ref, padcnt_ref, dest_ref, x_ref, g_ref, xs_ref, hbuf, zrow, sems, zsem):
    i = pl.program_id(0)
    n = pl.num_programs(0)
    slot = i % 2

    def zero_copy(row):
        return pltpu.make_async_copy(zrow.at[pl.ds(0, 1), :], xs_ref.at[pl.ds(row, 1), :], zsem)

    @pl.when(i == 0)
    def _():
        zrow[...] = jnp.zeros_like(zrow)
        for e in range(N_EXPERTS):
            def start(r, c, e=e):
                zero_copy(padstart_ref[e] + r).start()
                return c
            lax.fori_loop(0, padcnt_ref[e], start, 0)

    x = x_ref[...]
    ms = jnp.mean(x * x, axis=-1, keepdims=True)
    hbuf[slot] = x * lax.rsqrt(ms + EPS) * g_ref[...]

    def send(r, c):
        for k in range(TOP_K):
            d = dest_ref[0, 0, k * DSP_TB + r]
            pltpu.make_async_copy(hbuf.at[slot, pl.ds(r, 1), :], xs_ref.at[pl.ds(d, 1), :], sems.at[slot]).start()
        return c

    lax.fori_loop(0, DSP_TB, send, 0)

    def wait_slot(s):
        _row_block_wait(hbuf.at[s], xs_ref.at[pl.ds(0, DSP_TB), :], sems.at[s], TOP_K)

    @pl.when(i > 0)
    def _():
        wait_slot(1 - slot)

    @pl.when(i == n - 1)
    def _():
        wait_slot(slot)
        for e in range(N_EXPERTS):
            def done(r, c):
                zero_copy(0).wait()
                return c
            lax.fori_loop(0, padcnt_ref[e], done, 0)


def _dispatch(padstart, padcnt, dest_blocks, x1, g_ffn, n_rows):
    t = x1.shape[0]
    grid_spec = pltpu.PrefetchScalarGridSpec(
        num_scalar_prefetch=2,
        grid=(t // DSP_TB,),
        in_specs=[
            pl.BlockSpec((1, 1, TOP_K * DSP_TB), lambda i, *_: (i, 0, 0), memory_space=pltpu.SMEM),
            pl.BlockSpec((DSP_TB, D_MODEL), lambda i, *_: (i, 0)),
            pl.BlockSpec((1, D_MODEL), lambda i, *_: (0, 0)),
        ],
        out_specs=pl.BlockSpec(memory_space=pl.ANY),
        scratch_shapes=[
            pltpu.VMEM((2, DSP_TB, D_MODEL), F32),
            pltpu.VMEM((8, D_MODEL), F32),
            pltpu.SemaphoreType.DMA((2,)),
            pltpu.SemaphoreType.DMA(()),
        ],
    )
    return pl.pallas_call(
        _dispatch_kernel,
        grid_spec=grid_spec,
        out_shape=jax.ShapeDtypeStruct((n_rows, D_MODEL), F32),
        compiler_params=_params(("arbitrary",)),
        name="moe_dispatch",
    )(padstart, padcnt, dest_blocks, x1, g_ffn)


def _combine_kernel(dcur_ref, dnext_ref, x_ref, wt_ref, ys_ref, o_ref, buf, sems):
    i = pl.program_id(0)
    n = pl.num_programs(0)
    slot = i % 2

    def fetch(dref, s):
        def body(r, c):
            for k in range(TOP_K):
                d = dref[0, 0, k * CMB_TB + r]
                pltpu.make_async_copy(ys_ref.at[pl.ds(d, 1), :], buf.at[s, k, pl.ds(r, 1), :], sems.at[s]).start()
            return c
        lax.fori_loop(0, CMB_TB, body, 0)

    @pl.when(i == 0)
    def _():
        fetch(dcur_ref, 0)

    @pl.when(i + 1 < n)
    def _():
        fetch(dnext_ref, 1 - slot)

    _row_block_wait(ys_ref.at[pl.ds(0, CMB_TB), :], buf.at[slot, 0], sems.at[slot], TOP_K)
    acc = x_ref[...]
    for k in range(TOP_K):
        acc = acc + wt_ref[:, k:k + 1] * buf[slot, k]
    o_ref[...] = acc


def _combine(dest_blocks, x1, wgt_t, ys):
    t = x1.shape[0]
    nb = t // CMB_TB
    return pl.pallas_call(
        _combine_kernel,
        grid=(nb,),
        in_specs=[
            pl.BlockSpec((1, 1, TOP_K * CMB_TB), lambda i: (i, 0, 0), memory_space=pltpu.SMEM),
            pl.BlockSpec((1, 1, TOP_K * CMB_TB), lambda i: (jnp.minimum(i + 1, nb - 1), 0, 0),
                         memory_space=pltpu.SMEM),
            pl.BlockSpec((CMB_TB, D_MODEL), lambda i: (i, 0)),
            pl.BlockSpec((CMB_TB, TOP_K), lambda i: (i, 0)),
            pl.BlockSpec(memory_space=pl.ANY),
        ],
        out_specs=pl.BlockSpec((CMB_TB, D_MODEL), lambda i: (i, 0)),
        out_shape=jax.ShapeDtypeStruct((t, D_MODEL), F32),
        scratch_shapes=[
            pltpu.VMEM((2, TOP_K, CMB_TB, D_MODEL), F32),
            pltpu.SemaphoreType.DMA((2,)),
        ],
        compiler_params=_params(("arbitrary",)),
        name="moe_combine",
    )(dest_blocks, dest_blocks, x1, wgt_t, ys)


def _moe_kernel(be_ref, na_ref, x_ref, wg_ref, wu_ref, bg_ref, bu_ref, wd_ref, bd_ref, o_ref, acc_ref, xb_ref):
    b = pl.program_id(0)
    j = pl.program_id(1)
    nj = pl.num_programs(1)
    active = b < na_ref[0]

    @pl.when(jnp.logical_and(active, j == 0))
    def _():
        acc_ref[...] = jnp.zeros_like(acc_ref)
        xb_ref[...] = x_ref[...].astype(BF16)

    @pl.when(active)
    def _():
        x = xb_ref[...]
        gate = jnp.dot(x, wg_ref[...], preferred_element_type=F32) + bg_ref[...]
        up = jnp.dot(x, wu_ref[...], preferred_element_type=F32) + bu_ref[...]
        gate = jnp.minimum(gate, SWIGLU_LIMIT)
        up = jnp.clip(up, -SWIGLU_LIMIT, SWIGLU_LIMIT)
        act = (up + 1.0) * gate * jax.nn.sigmoid(SWIGLU_ALPHA * gate)
        acc_ref[...] += jnp.dot(act.astype(BF16), wd_ref[...], preferred_element_type=F32)

    @pl.when(jnp.logical_and(active, j == nj - 1))
    def _():
        o_ref[...] = acc_ref[...] + bd_ref[...]

    @pl.when(jnp.logical_and(jnp.logical_not(active), j == nj - 1))
    def _():
        o_ref[...] = jnp.zeros_like(o_ref)


def _moe(block_e, n_active, xs, w_gu, b_gu, w_down, b_down):
    n_pad = xs.shape[0]
    nfc = D_FF // MOE_FC
    grid_spec = pltpu.PrefetchScalarGridSpec(
        num_scalar_prefetch=2,
        grid=(n_pad // MOE_BM, nfc),
        in_specs=[
            pl.BlockSpec((MOE_BM, D_MODEL), lambda b, j, be, na: (jnp.minimum(b, na[0] - 1), 0)),
            pl.BlockSpec((None, D_MODEL, MOE_FC), lambda b, j, be, na: (be[b], 0, j)),
            pl.BlockSpec((None, D_MODEL, MOE_FC), lambda b, j, be, na: (be[b], 0, nfc + j)),
            pl.BlockSpec((None, 1, MOE_FC), lambda b, j, be, na: (be[b], 0, j)),
            pl.BlockSpec((None, 1, MOE_FC), lambda b, j, be, na: (be[b], 0, nfc + j)),
            pl.BlockSpec((None, MOE_FC, D_MODEL), lambda b, j, be, na: (be[b], j, 0)),
            pl.BlockSpec((None, 1, D_MODEL), lambda b, j, be, na: (be[b], 0, 0)),
        ],
        out_specs=pl.BlockSpec((MOE_BM, D_MODEL), lambda b, j, be, na: (b, 0)),
        scratch_shapes=[pltpu.VMEM((MOE_BM, D_MODEL), F32), pltpu.VMEM((MOE_BM, D_MODEL), BF16)],
    )
    return pl.pallas_call(
        _moe_kernel,
        grid_spec=grid_spec,
        out_shape=jax.ShapeDtypeStruct((n_pad, D_MODEL), F32),
        compiler_params=_params(("arbitrary", "arbitrary")),
        name="moe_experts",
    )(block_e, n_active, xs, w_gu, w_gu, b_gu, b_gu, w_down, b_down)


def _ple_kernel(xf_ref, xb_ref, g_ref, p_ref, wg_ref, wp_ref, o_ref, hn_ref):
    j = pl.program_id(1)

    @pl.when(j == 0)
    def _():
        x = xf_ref[...]
        ms = jnp.mean(x * x, axis=-1, keepdims=True)
        hn_ref[...] = (x * lax.rsqrt(ms + EPS) * g_ref[...]).astype(BF16)

    gate = jax.nn.sigmoid(jnp.dot(hn_ref[...], wg_ref[...], preferred_element_type=F32))
    pp = jnp.dot(p_ref[...], wp_ref[...], preferred_element_type=F32)
    o_ref[...] = xb_ref[...] + gate * pp


def _ple(x2, g_ple, p, w_gate, w_proj):
    t = x2.shape[0]
    return pl.pallas_call(
        _ple_kernel,
        grid=(t // PLE_TM, D_MODEL // PLE_TN),
        in_specs=[
            pl.BlockSpec((PLE_TM, D_MODEL), lambda i, j: (i, 0)),
            pl.BlockSpec((PLE_TM, PLE_TN), lambda i, j: (i, j)),
            pl.BlockSpec((1, D_MODEL), lambda i, j: (0, 0)),
            pl.BlockSpec((PLE_TM, PLE_DIM), lambda i, j: (i, 0)),
            pl.BlockSpec((D_MODEL, PLE_TN), lambda i, j: (0, j)),
            pl.BlockSpec((PLE_DIM, PLE_TN), lambda i, j: (0, j)),
        ],
        out_specs=pl.BlockSpec((PLE_TM, PLE_TN), lambda i, j: (i, j)),
        out_shape=jax.ShapeDtypeStruct((t, D_MODEL), F32),
        scratch_shapes=[pltpu.VMEM((PLE_TM, D_MODEL), BF16)],
        compiler_params=_params(("parallel", "arbitrary")),
        name="ple",
    )(x2, x2, g_ple, p, w_gate, w_proj)


def _rope_tables(seq_lens):
    half = HEAD_DIM // 2
    inv_freq = ROPE_THETA ** (-jnp.arange(half, dtype=F32) / half)
    cos_parts, sin_parts = [], []
    for s in seq_lens:
        ang = jnp.arange(s, dtype=F32)[:, None] * inv_freq[None, :]
        cos, sin = jnp.cos(ang), jnp.sin(ang)
        cos_parts.append(jnp.concatenate([cos, cos], axis=-1))
        sin_parts.append(jnp.concatenate([-sin, sin], axis=-1))
    return jnp.concatenate(cos_parts, axis=0), jnp.concatenate(sin_parts, axis=0)


def _seq_edge_flags(seq_lens):
    first, last = [], []
    for s in seq_lens:
        nb = s // ATT_QB
        first += [1] + [0] * (nb - 1)
        last += [0] * (nb - 1) + [1]
    return jnp.array(first, jnp.int32), jnp.array(last, jnp.int32)


def _layer(x, p, seq_lens, g_mix, w_in, q_norm, k_norm, sinks, g_sg_v, w_s, b_s, w_branch, w_o,
           g_ffn, w_router, b_router, w_gu, b_gu, w_down, b_down, g_ple, w_ple_gate, w_ple_proj):
    t = x.shape[0]
    q_end = Q_COLS
    kv_end = Q_COLS + 2 * KV_COLS
    w_in_r = jnp.concatenate([w_in[:, :q_end], w_in[:, kv_end:], w_in[:, q_end:kv_end]], axis=1).astype(BF16)
    cos_t, sin_t = _rope_tables(seq_lens)
    first, last = _seq_edge_flags(seq_lens)

    proj = _in_proj(x, g_mix[None, :], w_in_r, q_norm[None, :], k_norm[None, :], cos_t, sin_t)
    attn = _attention(proj, sinks, first, last)
    bs_b = jnp.broadcast_to(b_s[:, :, None], (SG_GROUPS, SG_CHUNK, SG_CHUNK))
    merged = _mix(attn, proj, w_branch[0].astype(BF16), w_branch[1].astype(BF16), w_s.astype(BF16), bs_b,
                  g_sg_v[None, :])
    x1 = _out_proj(merged, w_o.astype(BF16), x)

    idx, wgt, rank, cnt = _router(x1, g_ffn[None, :], w_router.T, b_router[:, None])

    counts = cnt[:, 0].astype(jnp.int32)
    padded = (counts + MOE_BM - 1) // MOE_BM * MOE_BM
    padded_end = jnp.cumsum(padded)
    padded_start = padded_end - padded
    dest = padded_start[idx] + rank
    n_blocks = (t * TOP_K) // MOE_BM + N_EXPERTS
    block_e = jnp.minimum(
        jnp.searchsorted(padded_end, jnp.arange(n_blocks, dtype=jnp.int32) * MOE_BM, side='right'),
        N_EXPERTS - 1).astype(jnp.int32)
    n_active = (padded_end[-1:] // MOE_BM).astype(jnp.int32)

    def token_blocks(tb):
        return dest.reshape(TOP_K, t // tb, tb).transpose(1, 0, 2).reshape(t // tb, 1, TOP_K * tb)

    xs = _dispatch(padded_start + counts, padded - counts, token_blocks(DSP_TB), x1, g_ffn[None, :],
                   n_blocks * MOE_BM)
    ys = _moe(block_e, n_active, xs, w_gu.astype(BF16), b_gu[:, None, :], w_down.astype(BF16),
              b_down[:, None, :])
    x2 = _combine(token_blocks(CMB_TB), x1, wgt.T, ys)

    return _ple(x2, g_ple[None, :], p.astype(BF16), w_ple_gate.astype(BF16), w_ple_proj.astype(BF16))


def kernel(x_prompt, x_sample, p_prompt, p_sample, g_mix, w_in, q_norm, k_norm, sinks, g_sg_v, w_s, b_s,
           w_branch, w_o, g_ffn, w_router, b_router, w_gu, b_gu, w_down, b_down, g_ple, w_ple_gate,
           w_ple_proj):
    depth = g_mix.shape[0]
    bp, sp, d = x_prompt.shape
    bs, ss, _ = x_sample.shape
    seq_lens = [sp] * bp + [ss] * bs
    n_p = bp * sp
    x = jnp.concatenate([x_prompt.reshape(n_p, d), x_sample.reshape(bs * ss, d)], axis=0)
    for l in range(depth):
        p = jnp.concatenate([p_prompt[l].reshape(n_p, PLE_DIM), p_sample[l].reshape(bs * ss, PLE_DIM)], axis=0)
        x = _layer(x, p, seq_lens, g_mix[l], w_in[l], q_norm[l], k_norm[l], sinks[l], g_sg_v[l], w_s[l], b_s[l],
                   w_branch[l], w_o[l], g_ffn[l], w_router[l], b_router[l], w_gu[l], b_gu[l], w_down[l],
                   b_down[l], g_ple[l], w_ple_gate[l], w_ple_proj[l])
    return x[:n_p].reshape(bp, sp, d), x[n_p:].reshape(bs, ss, d)
```

```python
import math

import jax
import jax.numpy as jnp
from jax import lax
from jax.experimental import pallas as pl
from jax.experimental.pallas import tpu as pltpu

F32 = jnp.float32
BF16 = jnp.bfloat16
U32 = jnp.uint32

D_MODEL = 2048
HALF = D_MODEL // 2
HEAD_DIM = 128
N_HEADS = 16
N_KV_HEADS = 4
GROUP = N_HEADS // N_KV_HEADS
WINDOW = 128
ROPE_THETA = 10000.0
SG_GROUPS = 16
SG_WIDTH = 2048
SG_CHUNK = 128
N_EXPERTS = 32
TOP_K = 4
D_FF = 2048
SWIGLU_LIMIT = 7.0
SWIGLU_ALPHA = 1.702
PLE_DIM = 256
EPS = 1e-6
NEG_INF = -1e30
LOG2E = math.log2(math.e)

Q_COLS = N_HEADS * HEAD_DIM
KV_COLS = N_KV_HEADS * HEAD_DIM
IN_COLS = Q_COLS + 2 * KV_COLS + 2 * SG_WIDTH + 2 * D_MODEL

COL_Q = 0
COL_U = Q_COLS
COL_SV = COL_U + SG_WIDTH
COL_GA = COL_SV + SG_WIDTH
COL_GS = COL_GA + D_MODEL
COL_K = COL_GS + D_MODEL
COL_V = COL_K + KV_COLS

VMEM_LIMIT = 56 * 1024 * 1024

IN_TM, IN_TN, IN_NC = 1024, 1024, 256
ATT_QB = 1024
MIX_TM, MIX_TN = 512, 512
OUT_TM, OUT_TN = 1024, 512
RT_TM = 512
MOE_BM, MOE_FC, MOE_SUB = 1024, 256, 256
DSP_TB = 256
CMB_TB = 256
PLE_TM, PLE_TN = 1024, 512


def _params(sem):
    return pltpu.CompilerParams(dimension_semantics=sem, vmem_limit_bytes=VMEM_LIMIT)


def _pack_pair(lo, hi):
    def rounded(x):
        b = lax.bitcast_convert_type(x, U32)
        return b + U32(0x7FFF) + ((b >> 16) & U32(1))
    return (rounded(lo) >> 16) | (rounded(hi) & U32(0xFFFF0000))


def _unpack_pair(u):
    lo = lax.bitcast_convert_type(u << 16, F32)
    hi = lax.bitcast_convert_type(u & U32(0xFFFF0000), F32)
    return lo, hi


def _in_proj_kernel(pos_ref, x_ref, g_ref, w_ref, qn_ref, kn_ref, cos_ref, sin_ref, o_ref, hn_ref):
    j = pl.program_id(1)

    @pl.when(j == 0)
    def _():
        x = x_ref[...]
        ms = jnp.mean(x * x, axis=-1, keepdims=True)
        hn_ref[...] = (x * lax.rsqrt(ms + EPS) * g_ref[...]).astype(BF16)

    def rope_heads(acc, col0, gain):
        cos = cos_ref[...]
        sin = sin_ref[...]
        for h in range(acc.shape[1] // HEAD_DIM):
            a = acc[:, h * HEAD_DIM:(h + 1) * HEAD_DIM]
            ms = jnp.mean(a * a, axis=-1, keepdims=True)
            an = a * lax.rsqrt(ms + EPS) * gain
            rot = pltpu.roll(an, HEAD_DIM // 2, axis=1)
            c0 = col0 + h * HEAD_DIM
            o_ref[:, c0:c0 + HEAD_DIM] = (an * cos + rot * sin).astype(BF16)

    def plain(fn):
        def epilogue(acc, col0):
            o_ref[:, col0:col0 + acc.shape[1]] = fn(acc).astype(BF16)
        return epilogue

    def tiles(epilogues):
        for c, epilogue in enumerate(epilogues):
            col0 = c * IN_NC
            acc = jnp.dot(hn_ref[...], w_ref[:, col0:col0 + IN_NC], preferred_element_type=F32)
            epilogue(acc, col0)

    n_chunks = IN_TN // IN_NC
    jq = COL_U // IN_TN
    ju = COL_GA // IN_TN
    jg = COL_K // IN_TN

    @pl.when(j < jq)
    def _():
        gain = qn_ref[...] * (HEAD_DIM ** -0.5 * LOG2E)
        tiles([lambda acc, col0: rope_heads(acc, col0, gain)] * n_chunks)

    @pl.when(jnp.logical_and(j >= jq, j < ju))
    def _():
        tiles([plain(jax.nn.gelu)] * n_chunks)

    @pl.when(jnp.logical_and(j >= ju, j < jg))
    def _():
        tiles([plain(jax.nn.sigmoid)] * n_chunks)

    @pl.when(j >= jg)
    def _():
        gain = kn_ref[...]
        nk = KV_COLS // IN_NC
        tiles([lambda acc, col0: rope_heads(acc, col0, gain)] * nk + [plain(lambda a: a)] * (n_chunks - nk))


def _in_proj(pos_blk, x, g_mix, w_in_r, q_norm, k_norm, cos_t, sin_t):
    t = x.shape[0]
    grid_spec = pltpu.PrefetchScalarGridSpec(
        num_scalar_prefetch=1,
        grid=(t // IN_TM, IN_COLS // IN_TN),
        in_specs=[
            pl.BlockSpec((IN_TM, D_MODEL), lambda i, j, pb: (i, 0)),
            pl.BlockSpec((1, D_MODEL), lambda i, j, pb: (0, 0)),
            pl.BlockSpec((D_MODEL, IN_TN), lambda i, j, pb: (0, j)),
            pl.BlockSpec((1, HEAD_DIM), lambda i, j, pb: (0, 0)),
            pl.BlockSpec((1, HEAD_DIM), lambda i, j, pb: (0, 0)),
            pl.BlockSpec((IN_TM, HEAD_DIM), lambda i, j, pb: (pb[i], 0)),
            pl.BlockSpec((IN_TM, HEAD_DIM), lambda i, j, pb: (pb[i], 0)),
        ],
        out_specs=pl.BlockSpec((IN_TM, IN_TN), lambda i, j, pb: (i, j)),
        scratch_shapes=[pltpu.VMEM((IN_TM, D_MODEL), BF16)],
    )
    return pl.pallas_call(
        _in_proj_kernel,
        grid_spec=grid_spec,
        out_shape=jax.ShapeDtypeStruct((t, IN_COLS), BF16),
        compiler_params=_params(("parallel", "arbitrary")),
        name="in_proj",
    )(pos_blk, x, g_mix, w_in_r, q_norm, k_norm, cos_t, sin_t)


def _attn_kernel(first_ref, last_ref, sinks_ref, q_ref, kp_ref, km_ref, kn_ref,
                 vp_ref, vm_ref, vn_ref, o_ref):
    i = pl.program_id(0)
    kv = pl.program_id(1)
    nqb = ATT_QB // WINDOW
    rows = GROUP * WINDOW
    kband = jnp.concatenate([kp_ref[...], km_ref[...], kn_ref[...]], axis=0)
    vband = jnp.concatenate([vp_ref[...], vm_ref[...], vn_ref[...]], axis=0)
    r = lax.broadcasted_iota(jnp.int32, (rows, 3 * WINDOW), 0) & (WINDOW - 1)
    c = lax.broadcasted_iota(jnp.int32, (rows, 3 * WINDOW), 1)
    d = c - r
    band_ok = jnp.logical_and(d >= 0, d <= 2 * WINDOW)
    lo = jnp.where(first_ref[i] != 0, WINDOW, 0)
    hi = jnp.where(last_ref[i] != 0, 2 * WINDOW, 3 * WINDOW)
    head = lax.broadcasted_iota(jnp.int32, (rows, 1), 0) >> (WINDOW.bit_length() - 1)
    sink = jnp.zeros((rows, 1), F32)
    for g in range(GROUP):
        sink = jnp.where(head == g, sinks_ref[kv * GROUP + g], sink)
    for b in range(nqb):
        ok = band_ok
        if b == 0:
            ok = jnp.logical_and(ok, c >= lo)
        if b == nqb - 1:
            ok = jnp.logical_and(ok, c < hi)
        kb = kband[b * WINDOW:(b + 3) * WINDOW]
        vb = vband[b * WINDOW:(b + 3) * WINDOW]
        qrows = slice(b * WINDOW, (b + 1) * WINDOW)
        qs = jnp.concatenate([q_ref[qrows, g * HEAD_DIM:(g + 1) * HEAD_DIM] for g in range(GROUP)], axis=0)
        s = lax.dot_general(qs, kb, (((1,), (1,)), ((), ())), preferred_element_type=F32)
        s = jnp.where(ok, s, NEG_INF)
        m = jnp.maximum(jnp.max(s, axis=-1, keepdims=True), sink)
        p = jnp.exp2(s - m)
        den = jnp.sum(p, axis=-1, keepdims=True) + jnp.exp2(sink - m)
        o = jnp.dot(p.astype(BF16), vb, preferred_element_type=F32) / den
        for g in range(GROUP):
            o_ref[qrows, g * HEAD_DIM:(g + 1) * HEAD_DIM] = o[g * WINDOW:(g + 1) * WINDOW].astype(BF16)


def _attention(proj, sinks, first, last):
    t = proj.shape[0]
    nqb = ATT_QB // WINDOW
    nrb = t // WINDOW
    kcol = COL_K // HEAD_DIM
    vcol = COL_V // HEAD_DIM
    qspec = pl.BlockSpec((ATT_QB, GROUP * HEAD_DIM), lambda i, kv, *_: (i, kv))

    def band_specs(col):
        return [
            pl.BlockSpec((WINDOW, HEAD_DIM), lambda i, kv, *_: (jnp.maximum(i * nqb - 1, 0), col + kv)),
            pl.BlockSpec((ATT_QB, HEAD_DIM), lambda i, kv, *_: (i, col + kv)),
            pl.BlockSpec((WINDOW, HEAD_DIM), lambda i, kv, *_: (jnp.minimum((i + 1) * nqb, nrb - 1), col + kv)),
        ]

    grid_spec = pltpu.PrefetchScalarGridSpec(
        num_scalar_prefetch=3,
        grid=(t // ATT_QB, N_KV_HEADS),
        in_specs=[qspec] + band_specs(kcol) + band_specs(vcol),
        out_specs=pl.BlockSpec((ATT_QB, GROUP * HEAD_DIM), lambda i, kv, *_: (i, kv)),
    )
    return pl.pallas_call(
        _attn_kernel,
        grid_spec=grid_spec,
        out_shape=jax.ShapeDtypeStruct((t, Q_COLS), BF16),
        compiler_params=_params(("parallel", "arbitrary")),
        name="window_attn",
    )(first, last, sinks, proj, proj, proj, proj, proj, proj, proj)


def _mix_kernel(attn_ref, u_ref, sv_ref, ga_ref, gs_ref, wa_ref, wb_ref, ws_ref, bs_ref,
                gsv_ref, o_ref, vn_ref, sg_ref):
    j = pl.program_id(1)

    @pl.when(j == 0)
    def _():
        sv = sv_ref[...].astype(F32)
        ms = jnp.mean(sv * sv, axis=-1, keepdims=True)
        vn_ref[...] = (sv * lax.rsqrt(ms + EPS) * gsv_ref[...]).astype(BF16)

        def chunk(c, carry):
            rows = pl.ds(pl.multiple_of(c * SG_CHUNK, SG_CHUNK), SG_CHUNK)
            for g in range(SG_GROUPS):
                cols = slice(g * SG_CHUNK, (g + 1) * SG_CHUNK)
                mixed = jnp.dot(ws_ref[g], vn_ref[rows, cols], preferred_element_type=F32) + bs_ref[g]
                sg_ref[rows, cols] = (u_ref[rows, cols].astype(F32) * mixed).astype(BF16)
            return carry

        lax.fori_loop(0, MIX_TM // SG_CHUNK, chunk, 0)

    pa = jnp.dot(attn_ref[...], wa_ref[...], preferred_element_type=F32)
    ps = jnp.dot(sg_ref[...], wb_ref[...], preferred_element_type=F32)
    merged = ga_ref[...].astype(F32) * pa + gs_ref[...].astype(F32) * ps
    o_ref[...] = merged.astype(BF16)


def _mix(attn, proj, wa, wb, ws, bs_b, gsv):
    t = attn.shape[0]
    grid = (t // MIX_TM, D_MODEL // MIX_TN)
    ga0 = COL_GA // MIX_TN
    gs0 = COL_GS // MIX_TN
    return pl.pallas_call(
        _mix_kernel,
        grid=grid,
        in_specs=[
            pl.BlockSpec((MIX_TM, Q_COLS), lambda i, j: (i, 0)),
            pl.BlockSpec((MIX_TM, SG_WIDTH), lambda i, j: (i, COL_U // SG_WIDTH)),
            pl.BlockSpec((MIX_TM, SG_WIDTH), lambda i, j: (i, COL_SV // SG_WIDTH)),
            pl.BlockSpec((MIX_TM, MIX_TN), lambda i, j: (i, ga0 + j)),
            pl.BlockSpec((MIX_TM, MIX_TN), lambda i, j: (i, gs0 + j)),
            pl.BlockSpec((Q_COLS, MIX_TN), lambda i, j: (0, j)),
            pl.BlockSpec((SG_WIDTH, MIX_TN), lambda i, j: (0, j)),
            pl.BlockSpec((SG_GROUPS, SG_CHUNK, SG_CHUNK), lambda i, j: (0, 0, 0)),
            pl.BlockSpec((SG_GROUPS, SG_CHUNK, SG_CHUNK), lambda i, j: (0, 0, 0)),
            pl.BlockSpec((1, SG_WIDTH), lambda i, j: (0, 0)),
        ],
        out_specs=pl.BlockSpec((MIX_TM, MIX_TN), lambda i, j: (i, j)),
        out_shape=jax.ShapeDtypeStruct((t, D_MODEL), BF16),
        scratch_shapes=[pltpu.VMEM((MIX_TM, SG_WIDTH), BF16), pltpu.VMEM((MIX_TM, SG_WIDTH), BF16)],
        compiler_params=_params(("parallel", "arbitrary")),
        name="branch_mix",
    )(attn, proj, proj, proj, proj, wa, wb, ws, bs_b, gsv)


def _out_proj_kernel(m_ref, w_ref, x_ref, o_ref):
    o_ref[...] = x_ref[...] + jnp.dot(m_ref[...], w_ref[...], preferred_element_type=F32)


def _out_proj(merged, w_o, x):
    t = x.shape[0]
    return pl.pallas_call(
        _out_proj_kernel,
        grid=(t // OUT_TM, D_MODEL // OUT_TN),
        in_specs=[
            pl.BlockSpec((OUT_TM, D_MODEL), lambda i, j: (i, 0)),
            pl.BlockSpec((D_MODEL, OUT_TN), lambda i, j: (0, j)),
            pl.BlockSpec((OUT_TM, OUT_TN), lambda i, j: (i, j)),
        ],
        out_specs=pl.BlockSpec((OUT_TM, OUT_TN), lambda i, j: (i, j)),
        out_shape=jax.ShapeDtypeStruct((t, D_MODEL), F32),
        compiler_params=_params(("parallel", "arbitrary")),
        name="out_proj",
    )(merged, w_o, x)


def _split_bf16(a):
    hi = a.astype(BF16)
    lo = (a - hi.astype(F32)).astype(BF16)
    return hi, lo


def _router_kernel(x_ref, g_ref, wr_ref, br_ref, idx_ref, wgt_ref, rank_ref, cnt_ref, carry_ref):
    step = pl.program_id(0)

    @pl.when(step == 0)
    def _():
        carry_ref[...] = jnp.zeros_like(carry_ref)

    x = x_ref[...]
    ms = jnp.mean(x * x, axis=-1, keepdims=True)
    h = x * lax.rsqrt(ms + EPS) * g_ref[...]

    h_hi, h_lo = _split_bf16(h)
    w_hi, w_lo = _split_bf16(wr_ref[...])
    dn = (((1,), (1,)), ((), ()))
    logits = (lax.dot_general(w_hi, h_hi, dn, preferred_element_type=F32)
              + lax.dot_general(w_hi, h_lo, dn, preferred_element_type=F32)
              + lax.dot_general(w_lo, h_hi, dn, preferred_element_type=F32))
    logits = logits + br_ref[...]

    e_iota = lax.broadcasted_iota(jnp.int32, (N_EXPERTS, RT_TM), 0)
    vals = logits
    tops, idxs = [], []
    onehot = jnp.zeros((N_EXPERTS, RT_TM), F32)
    for _ in range(TOP_K):
        m = jnp.max(vals, axis=0, keepdims=True)
        idx = jnp.min(jnp.where(vals == m, e_iota, N_EXPERTS), axis=0, keepdims=True)
        sel = e_iota == idx
        onehot = onehot + sel.astype(F32)
        vals = jnp.where(sel, -jnp.inf, vals)
        tops.append(m)
        idxs.append(idx)

    exps = [jnp.exp(v - tops[0]) for v in tops]
    den = exps[0] + exps[1] + exps[2] + exps[3]

    tr = lax.broadcasted_iota(jnp.int32, (RT_TM, RT_TM), 0)
    tc = lax.broadcasted_iota(jnp.int32, (RT_TM, RT_TM), 1)
    upper = jnp.where(tr < tc, 1.0, 0.0).astype(BF16)
    before = jnp.dot(onehot.astype(BF16), upper, preferred_element_type=F32) + carry_ref[:, 0:1]

    for k in range(TOP_K):
        sel = e_iota == idxs[k]
        rank = jnp.sum(jnp.where(sel, before, 0.0), axis=0, keepdims=True)
        idx_ref[k:k + 1, :] = idxs[k]
        wgt_ref[k:k + 1, :] = exps[k] / den
        rank_ref[k:k + 1, :] = rank.astype(jnp.int32)

    carry_ref[...] = carry_ref[...] + jnp.sum(onehot, axis=1, keepdims=True)
    cnt_ref[...] = carry_ref[...]


def _router(x1, g_ffn, w_router_t, b_router):
    t = x1.shape[0]
    return pl.pallas_call(
        _router_kernel,
        grid=(t // RT_TM,),
        in_specs=[
            pl.BlockSpec((RT_TM, D_MODEL), lambda i: (i, 0)),
            pl.BlockSpec((1, D_MODEL), lambda i: (0, 0)),
            pl.BlockSpec((N_EXPERTS, D_MODEL), lambda i: (0, 0)),
            pl.BlockSpec((N_EXPERTS, 1), lambda i: (0, 0)),
        ],
        out_specs=[
            pl.BlockSpec((TOP_K, RT_TM), lambda i: (0, i)),
            pl.BlockSpec((TOP_K, RT_TM), lambda i: (0, i)),
            pl.BlockSpec((TOP_K, RT_TM), lambda i: (0, i)),
            pl.BlockSpec((N_EXPERTS, 128), lambda i: (0, 0)),
        ],
        out_shape=[
            jax.ShapeDtypeStruct((TOP_K, t), jnp.int32),
            jax.ShapeDtypeStruct((TOP_K, t), F32),
            jax.ShapeDtypeStruct((TOP_K, t), jnp.int32),
            jax.ShapeDtypeStruct((N_EXPERTS, 128), F32),
        ],
        scratch_shapes=[pltpu.VMEM((N_EXPERTS, 128), F32)],
        compiler_params=_params(("arbitrary",)),
        name="router",
    )(x1, g_ffn, w_router_t, b_router)


def _row_block_wait(src, dst, sem, copies):
    for _ in range(copies):
        pltpu.make_async_copy(src, dst, sem).wait()


def _dispatch_kernel(padstart_ref, padcnt_ref, dest_ref, x_ref, g_ref, xs_ref, hbuf, zrow, sems, zsem):
    i = pl.program_id(0)
    n = pl.num_programs(0)
    slot = i % 2

    def zero_copy(row):
        return pltpu.make_async_copy(zrow.at[pl.ds(0, 1), :], xs_ref.at[pl.ds(row, 1), :], zsem)

    @pl.when(i == 0)
    def _():
        zrow[...] = jnp.zeros_like(zrow)
        for e in range(N_EXPERTS):
            def start(r, c, e=e):
                zero_copy(padstart_ref[e] + r).start()
                return c
            lax.fori_loop(0, padcnt_ref[e], start, 0)

    x = x_ref[...]
    ms = jnp.mean(x * x, axis=-1, keepdims=True)
    h = x * lax.rsqrt(ms + EPS) * g_ref[...]
    hbuf[slot] = _pack_pair(h[:, :HALF], h[:, HALF:])

    def send(r, c):
        for k in range(TOP_K):
            d = dest_ref[0, 0, k * DSP_TB + r]
            pltpu.make_async_copy(hbuf.at[slot, pl.ds(r, 1), :], xs_ref.at[pl.ds(d, 1), :], sems.at[slot]).start()
        return c

    lax.fori_loop(0, DSP_TB, send, 0)

    def wait_slot(s):
        _row_block_wait(hbuf.at[s], xs_ref.at[pl.ds(0, DSP_TB), :], sems.at[s], TOP_K)

    @pl.when(i > 0)
    def _():
        wait_slot(1 - slot)

    @pl.when(i == n - 1)
    def _():
        wait_slot(slot)
        for e in range(N_EXPERTS):
            def done(r, c):
                zero_copy(0).wait()
                return c
            lax.fori_loop(0, padcnt_ref[e], done, 0)


def _dispatch(padstart, padcnt, dest_blocks, x1, g_ffn, n_rows):
    t = x1.shape[0]
    grid_spec = pltpu.PrefetchScalarGridSpec(
        num_scalar_prefetch=2,
        grid=(t // DSP_TB,),
        in_specs=[
            pl.BlockSpec((1, 1, TOP_K * DSP_TB), lambda i, *_: (i, 0, 0), memory_space=pltpu.SMEM),
            pl.BlockSpec((DSP_TB, D_MODEL), lambda i, *_: (i, 0)),
            pl.BlockSpec((1, D_MODEL), lambda i, *_: (0, 0)),
        ],
        out_specs=pl.BlockSpec(memory_space=pl.ANY),
        scratch_shapes=[
            pltpu.VMEM((2, DSP_TB, HALF), U32),
            pltpu.VMEM((8, HALF), U32),
            pltpu.SemaphoreType.DMA((2,)),
            pltpu.SemaphoreType.DMA(()),
        ],
    )
    return pl.pallas_call(
        _dispatch_kernel,
        grid_spec=grid_spec,
        out_shape=jax.ShapeDtypeStruct((n_rows, HALF), U32),
        compiler_params=_params(("arbitrary",)),
        name="moe_dispatch",
    )(padstart, padcnt, dest_blocks, x1, g_ffn)


def _moe_kernel(be_ref, ns_ref, na_ref, x_ref, wg_ref, wu_ref, bg_ref, bu_ref, wd_ref, bd_ref, o_ref,
                acc_ref, xb_ref):
    b = pl.program_id(0)
    j = pl.program_id(1)
    nj = pl.num_programs(1)
    nsub = ns_ref[b]

    n_full = MOE_BM // MOE_SUB
    full = nsub == n_full
    whole = slice(None)

    def rows_of(s):
        return pl.ds(pl.multiple_of(s * MOE_SUB, MOE_SUB), MOE_SUB)

    def for_rows(fn):
        @pl.when(full)
        def _():
            fn(whole, MOE_BM)

        @pl.when(jnp.logical_not(full))
        def _():
            def body(s, c):
                fn(rows_of(s), MOE_SUB)
                return c
            lax.fori_loop(0, nsub, body, 0)

    def unpack(rows, m):
        lo, hi = _unpack_pair(x_ref[rows, :])
        xb_ref[0, rows, :] = lo.astype(BF16)
        xb_ref[1, rows, :] = hi.astype(BF16)
        acc_ref[rows, :] = jnp.zeros((m, D_MODEL), F32)

    def expert_mlp(rows, m):
        x0 = xb_ref[0, rows, :]
        x1 = xb_ref[1, rows, :]
        wg = wg_ref[...].astype(BF16)
        wu = wu_ref[...].astype(BF16)
        gate = (jnp.dot(x0, wg[:HALF], preferred_element_type=F32)
                + jnp.dot(x1, wg[HALF:], preferred_element_type=F32) + bg_ref[...])
        up = (jnp.dot(x0, wu[:HALF], preferred_element_type=F32)
              + jnp.dot(x1, wu[HALF:], preferred_element_type=F32) + bu_ref[...])
        gate = jnp.minimum(gate, SWIGLU_LIMIT)
        up = jnp.clip(up, -SWIGLU_LIMIT, SWIGLU_LIMIT)
        act = (up + 1.0) * gate * jax.nn.sigmoid(SWIGLU_ALPHA * gate)
        acc_ref[rows, :] += jnp.dot(act.astype(BF16), wd_ref[...].astype(BF16), preferred_element_type=F32)

    def finish(rows, m):
        y = acc_ref[rows, :] + bd_ref[...]
        o_ref[rows, :] = _pack_pair(y[:, :HALF], y[:, HALF:])

    @pl.when(j == 0)
    def _():
        for_rows(unpack)

    for_rows(expert_mlp)

    @pl.when(j == nj - 1)
    def _():
        for_rows(finish)

        def blank(s, c):
            o_ref[rows_of(s), :] = jnp.zeros((MOE_SUB, HALF), U32)
            return c
        lax.fori_loop(nsub, n_full, blank, 0)


def _moe(block_e, block_nsub, n_active, xs, w_gu, b_gu, w_down, b_down):
    n_rows = xs.shape[0]
    nfc = D_FF // MOE_FC

    def frozen_j(b, j, na):
        return jnp.where(b < na[0], j, nfc - 1)

    grid_spec = pltpu.PrefetchScalarGridSpec(
        num_scalar_prefetch=3,
        grid=(n_rows // MOE_BM, nfc),
        in_specs=[
            pl.BlockSpec((MOE_BM, HALF), lambda b, j, be, ns, na: (jnp.minimum(b, na[0] - 1), 0)),
            pl.BlockSpec((None, D_MODEL, MOE_FC), lambda b, j, be, ns, na: (be[b], 0, frozen_j(b, j, na))),
            pl.BlockSpec((None, D_MODEL, MOE_FC), lambda b, j, be, ns, na: (be[b], 0, nfc + frozen_j(b, j, na))),
            pl.BlockSpec((None, 1, MOE_FC), lambda b, j, be, ns, na: (be[b], 0, frozen_j(b, j, na))),
            pl.BlockSpec((None, 1, MOE_FC), lambda b, j, be, ns, na: (be[b], 0, nfc + frozen_j(b, j, na))),
            pl.BlockSpec((None, MOE_FC, D_MODEL), lambda b, j, be, ns, na: (be[b], frozen_j(b, j, na), 0)),
            pl.BlockSpec((None, 1, D_MODEL), lambda b, j, be, ns, na: (be[b], 0, 0)),
        ],
        out_specs=pl.BlockSpec((MOE_BM, HALF), lambda b, j, be, ns, na: (b, 0)),
        scratch_shapes=[pltpu.VMEM((MOE_BM, D_MODEL), F32), pltpu.VMEM((2, MOE_BM, HALF), BF16)],
    )
    return pl.pallas_call(
        _moe_kernel,
        grid_spec=grid_spec,
        out_shape=jax.ShapeDtypeStruct((n_rows, HALF), U32),
        compiler_params=_params(("arbitrary", "arbitrary")),
        name="moe_experts",
    )(block_e, block_nsub, n_active, xs, w_gu, w_gu, b_gu, b_gu, w_down, b_down)


def _combine_kernel(dcur_ref, dnext_ref, x_ref, wt_ref, ys_ref, o_ref, buf, sems):
    i = pl.program_id(0)
    n = pl.num_programs(0)
    slot = i % 2

    def fetch(dref, s):
        def body(r, c):
            for k in range(TOP_K):
                d = dref[0, 0, k * CMB_TB + r]
                pltpu.make_async_copy(ys_ref.at[pl.ds(d, 1), :], buf.at[s, k, pl.ds(r, 1), :], sems.at[s]).start()
            return c
        lax.fori_loop(0, CMB_TB, body, 0)

    @pl.when(i == 0)
    def _():
        fetch(dcur_ref, 0)

    @pl.when(i + 1 < n)
    def _():
        fetch(dnext_ref, 1 - slot)

    _row_block_wait(ys_ref.at[pl.ds(0, CMB_TB), :], buf.at[slot, 0], sems.at[slot], TOP_K)
    acc_lo = x_ref[:, :HALF]
    acc_hi = x_ref[:, HALF:]
    for k in range(TOP_K):
        lo, hi = _unpack_pair(buf[slot, k])
        w = wt_ref[:, k:k + 1]
        acc_lo = acc_lo + w * lo
        acc_hi = acc_hi + w * hi
    o_ref[:, :HALF] = acc_lo
    o_ref[:, HALF:] = acc_hi


def _combine(dest_blocks, x1, wgt_t, ys):
    t = x1.shape[0]
    nb = t // CMB_TB
    return pl.pallas_call(
        _combine_kernel,
        grid=(nb,),
        in_specs=[
            pl.BlockSpec((1, 1, TOP_K * CMB_TB), lambda i: (i, 0, 0), memory_space=pltpu.SMEM),
            pl.BlockSpec((1, 1, TOP_K * CMB_TB), lambda i: (jnp.minimum(i + 1, nb - 1), 0, 0),
                         memory_space=pltpu.SMEM),
            pl.BlockSpec((CMB_TB, D_MODEL), lambda i: (i, 0)),
            pl.BlockSpec((CMB_TB, TOP_K), lambda i: (i, 0)),
            pl.BlockSpec(memory_space=pl.ANY),
        ],
        out_specs=pl.BlockSpec((CMB_TB, D_MODEL), lambda i: (i, 0)),
        out_shape=jax.ShapeDtypeStruct((t, D_MODEL), F32),
        scratch_shapes=[
            pltpu.VMEM((2, TOP_K, CMB_TB, HALF), U32),
            pltpu.SemaphoreType.DMA((2,)),
        ],
        compiler_params=_params(("arbitrary",)),
        name="moe_combine",
    )(dest_blocks, dest_blocks, x1, wgt_t, ys)


def _ple_kernel(xf_ref, xb_ref, g_ref, p_ref, wg_ref, wp_ref, o_ref, hn_ref):
    j = pl.program_id(1)

    @pl.when(j == 0)
    def _():
        x = xf_ref[...]
        ms = jnp.mean(x * x, axis=-1, keepdims=True)
        hn_ref[...] = (x * lax.rsqrt(ms + EPS) * g_ref[...]).astype(BF16)

    gate = jax.nn.sigmoid(jnp.dot(hn_ref[...], wg_ref[...], preferred_element_type=F32))
    pp = jnp.dot(p_ref[...], wp_ref[...], preferred_element_type=F32)
    o_ref[...] = xb_ref[...] + gate * pp


def _ple(x2, g_ple, p, w_gate, w_proj):
    t = x2.shape[0]
    return pl.pallas_call(
        _ple_kernel,
        grid=(t // PLE_TM, D_MODEL // PLE_TN),
        in_specs=[
            pl.BlockSpec((PLE_TM, D_MODEL), lambda i, j: (i, 0)),
            pl.BlockSpec((PLE_TM, PLE_TN), lambda i, j: (i, j)),
            pl.BlockSpec((1, D_MODEL), lambda i, j: (0, 0)),
            pl.BlockSpec((PLE_TM, PLE_DIM), lambda i, j: (i, 0)),
            pl.BlockSpec((D_MODEL, PLE_TN), lambda i, j: (0, j)),
            pl.BlockSpec((PLE_DIM, PLE_TN), lambda i, j: (0, j)),
        ],
        out_specs=pl.BlockSpec((PLE_TM, PLE_TN), lambda i, j: (i, j)),
        out_shape=jax.ShapeDtypeStruct((t, D_MODEL), F32),
        scratch_shapes=[pltpu.VMEM((PLE_TM, D_MODEL), BF16)],
        compiler_params=_params(("parallel", "arbitrary")),
        name="ple",
    )(x2, x2, g_ple, p, w_gate, w_proj)


def _rope_tables(max_len):
    half = HEAD_DIM // 2
    inv_freq = ROPE_THETA ** (-jnp.arange(half, dtype=F32) / half)
    ang = jnp.arange(max_len, dtype=F32)[:, None] * inv_freq[None, :]
    cos, sin = jnp.cos(ang), jnp.sin(ang)
    return jnp.concatenate([cos, cos], axis=-1), jnp.concatenate([-sin, sin], axis=-1)


def _seq_block_tables(seq_lens):
    pos, first, last = [], [], []
    for s in seq_lens:
        pos += list(range(s // IN_TM))
        nb = s // ATT_QB
        first += [1] + [0] * (nb - 1)
        last += [0] * (nb - 1) + [1]
    return jnp.array(pos, jnp.int32), jnp.array(first, jnp.int32), jnp.array(last, jnp.int32)


def _layer(x, p, seq_lens, g_mix, w_in, q_norm, k_norm, sinks, g_sg_v, w_s, b_s, w_branch, w_o,
           g_ffn, w_router, b_router, w_gu, b_gu, w_down, b_down, g_ple, w_ple_gate, w_ple_proj):
    t = x.shape[0]
    q_end = Q_COLS
    kv_end = Q_COLS + 2 * KV_COLS
    w_in_r = jnp.concatenate([w_in[:, :q_end], w_in[:, kv_end:], w_in[:, q_end:kv_end]], axis=1).astype(BF16)
    cos_t, sin_t = _rope_tables(max(seq_lens))
    pos_blk, first, last = _seq_block_tables(seq_lens)

    proj = _in_proj(pos_blk, x, g_mix[None, :], w_in_r, q_norm[None, :], k_norm[None, :], cos_t, sin_t)
    attn = _attention(proj, sinks * LOG2E, first, last)
    bs_b = jnp.broadcast_to(b_s[:, :, None], (SG_GROUPS, SG_CHUNK, SG_CHUNK))
    merged = _mix(attn, proj, w_branch[0].astype(BF16), w_branch[1].astype(BF16), w_s.astype(BF16), bs_b,
                  g_sg_v[None, :])
    x1 = _out_proj(merged, w_o.astype(BF16), x)

    idx, wgt, rank, cnt = _router(x1, g_ffn[None, :], w_router.T, b_router[:, None])

    counts = cnt[:, 0].astype(jnp.int32)
    padded = (counts + MOE_BM - 1) // MOE_BM * MOE_BM
    padded_end = jnp.cumsum(padded)
    padded_start = padded_end - padded
    experts = jnp.arange(N_EXPERTS, dtype=jnp.int32)
    dest = rank + jnp.sum(jnp.where(idx[:, :, None] == experts, padded_start, 0), axis=-1)
    n_blocks = (t * TOP_K) // MOE_BM + N_EXPERTS
    block_row = jnp.arange(n_blocks, dtype=jnp.int32) * MOE_BM
    n_active = padded_end[-1] // MOE_BM
    block_e = jnp.minimum(jnp.sum(padded_end[None, :] <= block_row[:, None], axis=1), N_EXPERTS - 1)
    block_valid = jnp.clip(padded_start[block_e] + counts[block_e] - block_row, 0, MOE_BM)
    block_nsub = ((block_valid + MOE_SUB - 1) // MOE_SUB).astype(jnp.int32)
    block_e = jnp.where(block_row < padded_end[-1], block_e, block_e[n_active - 1]).astype(jnp.int32)

    def token_blocks(tb):
        return dest.reshape(TOP_K, t // tb, tb).transpose(1, 0, 2).reshape(t // tb, 1, TOP_K * tb)

    xs = _dispatch(padded_start + counts, (-counts) % MOE_SUB, token_blocks(DSP_TB), x1, g_ffn[None, :],
                   n_blocks * MOE_BM)
    ys = _moe(block_e, block_nsub, n_active[None].astype(jnp.int32), xs, w_gu, b_gu[:, None, :], w_down,
              b_down[:, None, :])
    x2 = _combine(token_blocks(CMB_TB), x1, wgt.T, ys)

    return _ple(x2, g_ple[None, :], p.astype(BF16), w_ple_gate.astype(BF16), w_ple_proj.astype(BF16))


def kernel(x_prompt, x_sample, p_prompt, p_sample, g_mix, w_in, q_norm, k_norm, sinks, g_sg_v, w_s, b_s,
           w_branch, w_o, g_ffn, w_router, b_router, w_gu, b_gu, w_down, b_down, g_ple, w_ple_gate,
           w_ple_proj):
    depth = g_mix.shape[0]
    bp, sp, d = x_prompt.shape
    bs, ss, _ = x_sample.shape
    seq_lens = [sp] * bp + [ss] * bs
    n_p = bp * sp
    x = jnp.concatenate([x_prompt.reshape(n_p, d), x_sample.reshape(bs * ss, d)], axis=0)
    for l in range(depth):
        p = jnp.concatenate([p_prompt[l].reshape(n_p, PLE_DIM), p_sample[l].reshape(bs * ss, PLE_DIM)], axis=0)
        x = _layer(x, p, seq_lens, g_mix[l], w_in[l], q_norm[l], k_norm[l], sinks[l], g_sg_v[l], w_s[l], b_s[l],
                   w_branch[l], w_o[l], g_ffn[l], w_router[l], b_router[l], w_gu[l], b_gu[l], w_down[l],
                   b_down[l], g_ple[l], w_ple_gate[l], w_ple_proj[l])
    return x[:n_p].reshape(bp, sp, d), x[n_p:].reshape(bs, ss, d)
```

```python
import functools
import math

import jax
import jax.numpy as jnp
import numpy as np
from jax import lax
from jax.experimental import pallas as pl
from jax.experimental.pallas import tpu as pltpu

F32 = jnp.float32
BF16 = jnp.bfloat16
U32 = jnp.uint32

D_MODEL = 2048
HALF = D_MODEL // 2
HEAD_DIM = 128
N_HEADS = 16
N_KV_HEADS = 4
GROUP = N_HEADS // N_KV_HEADS
WINDOW = 128
ROPE_THETA = 10000.0
SG_GROUPS = 16
SG_WIDTH = 2048
SG_CHUNK = 128
N_EXPERTS = 32
TOP_K = 4
D_FF = 2048
SWIGLU_LIMIT = 7.0
SWIGLU_ALPHA = 1.702
PLE_DIM = 256
EPS = 1e-6
NEG_INF = -1e30
LOG2E = math.log2(math.e)

Q_COLS = N_HEADS * HEAD_DIM
KV_COLS = N_KV_HEADS * HEAD_DIM
IN_COLS = Q_COLS + 2 * KV_COLS + 2 * SG_WIDTH + 2 * D_MODEL

COL_Q = 0
COL_U = Q_COLS
COL_SV = COL_U + SG_WIDTH
COL_GA = COL_SV + SG_WIDTH
COL_GS = COL_GA + D_MODEL
COL_K = COL_GS + D_MODEL
COL_V = COL_K + KV_COLS

VMEM_LIMIT = 56 * 1024 * 1024

IN_TM, IN_TN, IN_NC = 1024, 1024, 256
ATT_QB = 1024
MIX_TM, MIX_TN = 512, 512
OUT_TM, OUT_TN = 1024, 512
RT_TM = 512
MOE_BM, MOE_FC, MOE_SUB = 1024, 256, 256
DSP_TB = 256
CMB_TB = 256
PLE_TM, PLE_TN = 1024, 512


def _params(sem):
    return pltpu.CompilerParams(dimension_semantics=sem, vmem_limit_bytes=VMEM_LIMIT)


def _pack_pair(lo, hi):
    def rounded_bits(x):
        return lax.bitcast_convert_type(x.astype(BF16).astype(F32), U32)
    return (rounded_bits(lo) >> 16) | rounded_bits(hi)


def _unpack_pair(u):
    lo = lax.bitcast_convert_type(u << 16, F32)
    hi = lax.bitcast_convert_type(u & U32(0xFFFF0000), F32)
    return lo, hi


def _in_proj_kernel(pos_ref, xp_ref, xs_ref, g_ref, w_ref, qg_ref, kg_ref, cos_ref, sin_ref, hsum_ref, rot_ref,
                    o_ref, hn_ref, *, n_prompt_blocks):
    i = pl.program_id(0)
    j = pl.program_id(1)

    def normalize(x_ref):
        x = x_ref[...]
        ms = jnp.mean(x * x, axis=-1, keepdims=True)
        hn_ref[...] = (x * lax.rsqrt(ms + EPS) * g_ref[...]).astype(BF16)

    @pl.when(jnp.logical_and(j == 0, i < n_prompt_blocks))
    def _():
        normalize(xp_ref)

    @pl.when(jnp.logical_and(j == 0, i >= n_prompt_blocks))
    def _():
        normalize(xs_ref)

    def rope_heads(gain):
        def epilogue(acc, col0):
            ms = jnp.dot((acc * acc).astype(BF16), hsum_ref[...], preferred_element_type=F32)
            an = acc * lax.rsqrt(ms + EPS) * gain
            rot = jnp.dot(an.astype(BF16), rot_ref[...], preferred_element_type=F32)
            o_ref[:, col0:col0 + IN_NC] = (an * cos_ref[...] + rot * sin_ref[...]).astype(BF16)
        return epilogue

    def plain(fn):
        def epilogue(acc, col0):
            o_ref[:, col0:col0 + IN_NC] = fn(acc).astype(BF16)
        return epilogue

    def tiles(epilogues):
        def chunk_dot(c):
            return jnp.dot(hn_ref[...], w_ref[:, c * IN_NC:(c + 1) * IN_NC], preferred_element_type=F32)

        acc = chunk_dot(0)
        for c, epilogue in enumerate(epilogues):
            nxt = chunk_dot(c + 1) if c + 1 < len(epilogues) else None
            epilogue(acc, c * IN_NC)
            acc = nxt

    n_chunks = IN_TN // IN_NC
    jq = COL_U // IN_TN
    ju = COL_GA // IN_TN
    jg = COL_K // IN_TN

    @pl.when(j < jq)
    def _():
        tiles([rope_heads(qg_ref[...])] * n_chunks)

    @pl.when(jnp.logical_and(j >= jq, j < ju))
    def _():
        tiles([plain(jax.nn.gelu)] * n_chunks)

    @pl.when(jnp.logical_and(j >= ju, j < jg))
    def _():
        tiles([plain(jax.nn.sigmoid)] * n_chunks)

    @pl.when(j >= jg)
    def _():
        nk = KV_COLS // IN_NC
        tiles([rope_heads(kg_ref[...])] * nk + [plain(lambda a: a)] * (n_chunks - nk))


def _in_proj(pos_blk, x_p, x_s, g_mix, w_in_r, q_gain, k_gain, cos_t, sin_t, head_sum, head_rot):
    npb = x_p.shape[0] // IN_TM
    t = x_p.shape[0] + x_s.shape[0]
    const = lambda i, j, pb: (0, 0)
    grid_spec = pltpu.PrefetchScalarGridSpec(
        num_scalar_prefetch=1,
        grid=(t // IN_TM, IN_COLS // IN_TN),
        in_specs=[
            pl.BlockSpec((IN_TM, D_MODEL), lambda i, j, pb: (jnp.minimum(i, npb - 1), 0)),
            pl.BlockSpec((IN_TM, D_MODEL), lambda i, j, pb: (jnp.maximum(i - npb, 0), 0)),
            pl.BlockSpec((1, D_MODEL), const),
            pl.BlockSpec((D_MODEL, IN_TN), lambda i, j, pb: (0, j)),
            pl.BlockSpec((1, IN_NC), const),
            pl.BlockSpec((1, IN_NC), const),
            pl.BlockSpec((IN_TM, IN_NC), lambda i, j, pb: (pb[i], 0)),
            pl.BlockSpec((IN_TM, IN_NC), lambda i, j, pb: (pb[i], 0)),
            pl.BlockSpec((IN_NC, IN_NC), const),
            pl.BlockSpec((IN_NC, IN_NC), const),
        ],
        out_specs=pl.BlockSpec((IN_TM, IN_TN), lambda i, j, pb: (i, j)),
        scratch_shapes=[pltpu.VMEM((IN_TM, D_MODEL), BF16)],
    )
    return pl.pallas_call(
        functools.partial(_in_proj_kernel, n_prompt_blocks=npb),
        grid_spec=grid_spec,
        out_shape=jax.ShapeDtypeStruct((t, IN_COLS), BF16),
        compiler_params=_params(("parallel", "arbitrary")),
        name="in_proj",
    )(pos_blk, x_p, x_s, g_mix, w_in_r, q_gain, k_gain, cos_t, sin_t, head_sum, head_rot)


def _attn_kernel(first_ref, last_ref, sinks_ref, q_ref, kp_ref, km_ref, kn_ref,
                 vp_ref, vm_ref, vn_ref, o_ref):
    i = pl.program_id(0)
    kv = pl.program_id(1)
    nqb = ATT_QB // WINDOW
    rows = GROUP * WINDOW
    kband = jnp.concatenate([kp_ref[...], km_ref[...], kn_ref[...]], axis=0)
    vband = jnp.concatenate([vp_ref[...], vm_ref[...], vn_ref[...]], axis=0)
    r = lax.broadcasted_iota(jnp.int32, (rows, 3 * WINDOW), 0) & (WINDOW - 1)
    c = lax.broadcasted_iota(jnp.int32, (rows, 3 * WINDOW), 1)
    d = c - r
    band_ok = jnp.logical_and(d >= 0, d <= 2 * WINDOW)
    lo = jnp.where(first_ref[i] != 0, WINDOW, 0)
    hi = jnp.where(last_ref[i] != 0, 2 * WINDOW, 3 * WINDOW)
    head = lax.broadcasted_iota(jnp.int32, (rows, 1), 0) >> (WINDOW.bit_length() - 1)
    sink = jnp.zeros((rows, 1), F32)
    for g in range(GROUP):
        sink = jnp.where(head == g, sinks_ref[kv * GROUP + g], sink)
    for b in range(nqb):
        ok = band_ok
        if b == 0:
            ok = jnp.logical_and(ok, c >= lo)
        if b == nqb - 1:
            ok = jnp.logical_and(ok, c < hi)
        kb = kband[b * WINDOW:(b + 3) * WINDOW]
        vb = vband[b * WINDOW:(b + 3) * WINDOW]
        qrows = slice(b * WINDOW, (b + 1) * WINDOW)
        qs = jnp.concatenate([q_ref[qrows, g * HEAD_DIM:(g + 1) * HEAD_DIM] for g in range(GROUP)], axis=0)
        s = lax.dot_general(qs, kb, (((1,), (1,)), ((), ())), preferred_element_type=F32)
        s = jnp.where(ok, s, NEG_INF)
        m = jnp.maximum(jnp.max(s, axis=-1, keepdims=True), sink)
        p = jnp.exp2(s - m)
        den = jnp.sum(p, axis=-1, keepdims=True) + jnp.exp2(sink - m)
        o = jnp.dot(p.astype(BF16), vb, preferred_element_type=F32) / den
        for g in range(GROUP):
            o_ref[qrows, g * HEAD_DIM:(g + 1) * HEAD_DIM] = o[g * WINDOW:(g + 1) * WINDOW].astype(BF16)


def _attention(proj, sinks, first, last):
    t = proj.shape[0]
    nqb = ATT_QB // WINDOW
    nrb = t // WINDOW
    kcol = COL_K // HEAD_DIM
    vcol = COL_V // HEAD_DIM
    qspec = pl.BlockSpec((ATT_QB, GROUP * HEAD_DIM), lambda i, kv, *_: (i, kv))

    def band_specs(col):
        return [
            pl.BlockSpec((WINDOW, HEAD_DIM), lambda i, kv, *_: (jnp.maximum(i * nqb - 1, 0), col + kv)),
            pl.BlockSpec((ATT_QB, HEAD_DIM), lambda i, kv, *_: (i, col + kv)),
            pl.BlockSpec((WINDOW, HEAD_DIM), lambda i, kv, *_: (jnp.minimum((i + 1) * nqb, nrb - 1), col + kv)),
        ]

    grid_spec = pltpu.PrefetchScalarGridSpec(
        num_scalar_prefetch=3,
        grid=(t // ATT_QB, N_KV_HEADS),
        in_specs=[qspec] + band_specs(kcol) + band_specs(vcol),
        out_specs=pl.BlockSpec((ATT_QB, GROUP * HEAD_DIM), lambda i, kv, *_: (i, kv)),
    )
    return pl.pallas_call(
        _attn_kernel,
        grid_spec=grid_spec,
        out_shape=jax.ShapeDtypeStruct((t, Q_COLS), BF16),
        compiler_params=_params(("parallel", "arbitrary")),
        name="window_attn",
    )(first, last, sinks, proj, proj, proj, proj, proj, proj, proj)


def _mix_kernel(attn_ref, u_ref, sv_ref, ga_ref, gs_ref, wa_ref, wb_ref, ws_ref, bs_ref,
                gsv_ref, o_ref, vn_ref, sg_ref):
    j = pl.program_id(1)

    @pl.when(j == 0)
    def _():
        sv = sv_ref[...].astype(F32)
        ms = jnp.mean(sv * sv, axis=-1, keepdims=True)
        vn_ref[...] = (sv * lax.rsqrt(ms + EPS) * gsv_ref[...]).astype(BF16)

        def chunk(c, carry):
            rows = pl.ds(pl.multiple_of(c * SG_CHUNK, SG_CHUNK), SG_CHUNK)
            for g in range(SG_GROUPS):
                cols = slice(g * SG_CHUNK, (g + 1) * SG_CHUNK)
                mixed = jnp.dot(ws_ref[g], vn_ref[rows, cols], preferred_element_type=F32) + bs_ref[g]
                sg_ref[rows, cols] = (u_ref[rows, cols].astype(F32) * mixed).astype(BF16)
            return carry

        lax.fori_loop(0, MIX_TM // SG_CHUNK, chunk, 0)

    pa = jnp.dot(attn_ref[...], wa_ref[...], preferred_element_type=F32)
    ps = jnp.dot(sg_ref[...], wb_ref[...], preferred_element_type=F32)
    merged = ga_ref[...].astype(F32) * pa + gs_ref[...].astype(F32) * ps
    o_ref[...] = merged.astype(BF16)


def _mix(attn, proj, wa, wb, ws, bs_b, gsv):
    t = attn.shape[0]
    grid = (t // MIX_TM, D_MODEL // MIX_TN)
    ga0 = COL_GA // MIX_TN
    gs0 = COL_GS // MIX_TN
    return pl.pallas_call(
        _mix_kernel,
        grid=grid,
        in_specs=[
            pl.BlockSpec((MIX_TM, Q_COLS), lambda i, j: (i, 0)),
            pl.BlockSpec((MIX_TM, SG_WIDTH), lambda i, j: (i, COL_U // SG_WIDTH)),
            pl.BlockSpec((MIX_TM, SG_WIDTH), lambda i, j: (i, COL_SV // SG_WIDTH)),
            pl.BlockSpec((MIX_TM, MIX_TN), lambda i, j: (i, ga0 + j)),
            pl.BlockSpec((MIX_TM, MIX_TN), lambda i, j: (i, gs0 + j)),
            pl.BlockSpec((Q_COLS, MIX_TN), lambda i, j: (0, j)),
            pl.BlockSpec((SG_WIDTH, MIX_TN), lambda i, j: (0, j)),
            pl.BlockSpec((SG_GROUPS, SG_CHUNK, SG_CHUNK), lambda i, j: (0, 0, 0)),
            pl.BlockSpec((SG_GROUPS, SG_CHUNK, SG_CHUNK), lambda i, j: (0, 0, 0)),
            pl.BlockSpec((1, SG_WIDTH), lambda i, j: (0, 0)),
        ],
        out_specs=pl.BlockSpec((MIX_TM, MIX_TN), lambda i, j: (i, j)),
        out_shape=jax.ShapeDtypeStruct((t, D_MODEL), BF16),
        scratch_shapes=[pltpu.VMEM((MIX_TM, SG_WIDTH), BF16), pltpu.VMEM((MIX_TM, SG_WIDTH), BF16)],
        compiler_params=_params(("parallel", "arbitrary")),
        name="branch_mix",
    )(attn, proj, proj, proj, proj, wa, wb, ws, bs_b, gsv)


def _out_proj_kernel(m_ref, w_ref, xp_ref, xs_ref, o_ref, *, n_prompt_blocks):
    i = pl.program_id(0)
    y = jnp.dot(m_ref[...], w_ref[...], preferred_element_type=F32)

    @pl.when(i < n_prompt_blocks)
    def _():
        o_ref[...] = xp_ref[...] + y

    @pl.when(i >= n_prompt_blocks)
    def _():
        o_ref[...] = xs_ref[...] + y


def _out_proj(merged, w_o, x_p, x_s):
    t = merged.shape[0]
    npb = x_p.shape[0] // OUT_TM
    nj = D_MODEL // OUT_TN
    return pl.pallas_call(
        functools.partial(_out_proj_kernel, n_prompt_blocks=npb),
        grid=(t // OUT_TM, nj),
        in_specs=[
            pl.BlockSpec((OUT_TM, D_MODEL), lambda i, j: (i, 0)),
            pl.BlockSpec((D_MODEL, OUT_TN), lambda i, j: (0, j)),
            pl.BlockSpec((OUT_TM, OUT_TN), lambda i, j: (jnp.minimum(i, npb - 1), jnp.where(i < npb, j, nj - 1))),
            pl.BlockSpec((OUT_TM, OUT_TN), lambda i, j: (jnp.maximum(i - npb, 0), jnp.where(i < npb, 0, j))),
        ],
        out_specs=pl.BlockSpec((OUT_TM, OUT_TN), lambda i, j: (i, j)),
        out_shape=jax.ShapeDtypeStruct((t, D_MODEL), F32),
        compiler_params=_params(("parallel", "arbitrary")),
        name="out_proj",
    )(merged, w_o, x_p, x_s)


def _split_bf16(a):
    hi = a.astype(BF16)
    lo = (a - hi.astype(F32)).astype(BF16)
    return hi, lo


def _router_kernel(x_ref, g_ref, wr_ref, br_ref, idx_ref, wgt_ref, rank_ref, cnt_ref, carry_ref):
    step = pl.program_id(0)

    @pl.when(step == 0)
    def _():
        carry_ref[...] = jnp.zeros_like(carry_ref)

    x = x_ref[...]
    ms = jnp.mean(x * x, axis=-1, keepdims=True)
    h = x * lax.rsqrt(ms + EPS) * g_ref[...]

    h_hi, h_lo = _split_bf16(h)
    w_hi, w_lo = _split_bf16(wr_ref[...])
    dn = (((1,), (1,)), ((), ()))
    logits = (lax.dot_general(w_hi, h_hi, dn, preferred_element_type=F32)
              + lax.dot_general(w_hi, h_lo, dn, preferred_element_type=F32)
              + lax.dot_general(w_lo, h_hi, dn, preferred_element_type=F32))
    logits = logits + br_ref[...]

    e_iota = lax.broadcasted_iota(jnp.int32, (N_EXPERTS, RT_TM), 0)
    vals = logits
    tops, idxs = [], []
    onehot = jnp.zeros((N_EXPERTS, RT_TM), F32)
    for _ in range(TOP_K):
        m = jnp.max(vals, axis=0, keepdims=True)
        idx = jnp.min(jnp.where(vals == m, e_iota, N_EXPERTS), axis=0, keepdims=True)
        sel = e_iota == idx
        onehot = onehot + sel.astype(F32)
        vals = jnp.where(sel, -jnp.inf, vals)
        tops.append(m)
        idxs.append(idx)

    exps = [jnp.exp(v - tops[0]) for v in tops]
    den = exps[0] + exps[1] + exps[2] + exps[3]

    tr = lax.broadcasted_iota(jnp.int32, (RT_TM, RT_TM), 0)
    tc = lax.broadcasted_iota(jnp.int32, (RT_TM, RT_TM), 1)
    upper = jnp.where(tr < tc, 1.0, 0.0).astype(BF16)
    before = jnp.dot(onehot.astype(BF16), upper, preferred_element_type=F32) + carry_ref[:, 0:1]

    for k in range(TOP_K):
        sel = e_iota == idxs[k]
        rank = jnp.sum(jnp.where(sel, before, 0.0), axis=0, keepdims=True)
        idx_ref[k:k + 1, :] = idxs[k]
        wgt_ref[k:k + 1, :] = exps[k] / den
        rank_ref[k:k + 1, :] = rank.astype(jnp.int32)

    carry_ref[...] = carry_ref[...] + jnp.sum(onehot, axis=1, keepdims=True)
    cnt_ref[...] = carry_ref[...]


def _router(x1, g_ffn, w_router_t, b_router):
    t = x1.shape[0]
    return pl.pallas_call(
        _router_kernel,
        grid=(t // RT_TM,),
        in_specs=[
            pl.BlockSpec((RT_TM, D_MODEL), lambda i: (i, 0)),
            pl.BlockSpec((1, D_MODEL), lambda i: (0, 0)),
            pl.BlockSpec((N_EXPERTS, D_MODEL), lambda i: (0, 0)),
            pl.BlockSpec((N_EXPERTS, 1), lambda i: (0, 0)),
        ],
        out_specs=[
            pl.BlockSpec((TOP_K, RT_TM), lambda i: (0, i)),
            pl.BlockSpec((TOP_K, RT_TM), lambda i: (0, i)),
            pl.BlockSpec((TOP_K, RT_TM), lambda i: (0, i)),
            pl.BlockSpec((N_EXPERTS, 128), lambda i: (0, 0)),
        ],
        out_shape=[
            jax.ShapeDtypeStruct((TOP_K, t), jnp.int32),
            jax.ShapeDtypeStruct((TOP_K, t), F32),
            jax.ShapeDtypeStruct((TOP_K, t), jnp.int32),
            jax.ShapeDtypeStruct((N_EXPERTS, 128), F32),
        ],
        scratch_shapes=[pltpu.VMEM((N_EXPERTS, 128), F32)],
        compiler_params=_params(("arbitrary",)),
        name="router",
    )(x1, g_ffn, w_router_t, b_router)


def _row_block_wait(src, dst, sem, copies):
    for _ in range(copies):
        pltpu.make_async_copy(src, dst, sem).wait()


def _dispatch_kernel(padstart_ref, padcnt_ref, dest_ref, x_ref, g_ref, xs_ref, hbuf, zrow, sems, zsem):
    i = pl.program_id(0)
    n = pl.num_programs(0)
    slot = i % 2

    def zero_copy(row):
        return pltpu.make_async_copy(zrow.at[pl.ds(0, 1), :], xs_ref.at[pl.ds(row, 1), :], zsem)

    @pl.when(i == 0)
    def _():
        zrow[...] = jnp.zeros_like(zrow)
        for e in range(N_EXPERTS):
            def start(r, c, e=e):
                zero_copy(padstart_ref[e] + r).start()
                return c
            lax.fori_loop(0, padcnt_ref[e], start, 0)

    x = x_ref[...]
    ms = jnp.mean(x * x, axis=-1, keepdims=True)
    h = x * lax.rsqrt(ms + EPS) * g_ref[...]
    hbuf[slot] = _pack_pair(h[:, :HALF], h[:, HALF:])

    def send(r, c):
        for k in range(TOP_K):
            d = dest_ref[0, 0, k * DSP_TB + r]
            pltpu.make_async_copy(hbuf.at[slot, pl.ds(r, 1), :], xs_ref.at[pl.ds(d, 1), :],
                                  sems.at[slot]).start(priority=k % 2)
        return c

    lax.fori_loop(0, DSP_TB, send, 0)

    def wait_slot(s):
        _row_block_wait(hbuf.at[s], xs_ref.at[pl.ds(0, DSP_TB), :], sems.at[s], TOP_K)

    @pl.when(i > 0)
    def _():
        wait_slot(1 - slot)

    @pl.when(i == n - 1)
    def _():
        wait_slot(slot)
        for e in range(N_EXPERTS):
            def done(r, c):
                zero_copy(0).wait()
                return c
            lax.fori_loop(0, padcnt_ref[e], done, 0)


def _dispatch(padstart, padcnt, dest_blocks, x1, g_ffn, n_rows):
    t = x1.shape[0]
    grid_spec = pltpu.PrefetchScalarGridSpec(
        num_scalar_prefetch=2,
        grid=(t // DSP_TB,),
        in_specs=[
            pl.BlockSpec((1, 1, TOP_K * DSP_TB), lambda i, *_: (i, 0, 0), memory_space=pltpu.SMEM),
            pl.BlockSpec((DSP_TB, D_MODEL), lambda i, *_: (i, 0)),
            pl.BlockSpec((1, D_MODEL), lambda i, *_: (0, 0)),
        ],
        out_specs=pl.BlockSpec(memory_space=pl.ANY),
        scratch_shapes=[
            pltpu.VMEM((2, DSP_TB, HALF), U32),
            pltpu.VMEM((8, HALF), U32),
            pltpu.SemaphoreType.DMA((2,)),
            pltpu.SemaphoreType.DMA(()),
        ],
    )
    return pl.pallas_call(
        _dispatch_kernel,
        grid_spec=grid_spec,
        out_shape=jax.ShapeDtypeStruct((n_rows, HALF), U32),
        compiler_params=_params(("arbitrary",)),
        name="moe_dispatch",
    )(padstart, padcnt, dest_blocks, x1, g_ffn)


def _moe_kernel(be_ref, ns_ref, na_ref, x_ref, wg_ref, wu_ref, bg_ref, bu_ref, wd_ref, bd_ref, o_ref,
                acc_ref, xb_ref):
    b = pl.program_id(0)
    j = pl.program_id(1)
    nj = pl.num_programs(1)
    nsub = ns_ref[b]

    n_full = MOE_BM // MOE_SUB
    full = nsub == n_full
    whole = slice(None)

    def rows_of(s):
        return pl.ds(pl.multiple_of(s * MOE_SUB, MOE_SUB), MOE_SUB)

    def for_rows(fn):
        @pl.when(full)
        def _():
            fn(whole, MOE_BM)

        @pl.when(jnp.logical_not(full))
        def _():
            def body(s, c):
                fn(rows_of(s), MOE_SUB)
                return c
            lax.fori_loop(0, nsub, body, 0)

    def unpack(rows, m):
        lo, hi = _unpack_pair(x_ref[rows, :])
        xb_ref[0, rows, :] = lo.astype(BF16)
        xb_ref[1, rows, :] = hi.astype(BF16)
        acc_ref[rows, :] = jnp.zeros((m, D_MODEL), F32)

    def expert_mlp(rows, m):
        x0 = xb_ref[0, rows, :]
        x1 = xb_ref[1, rows, :]
        wg = wg_ref[...].astype(BF16)
        wu = wu_ref[...].astype(BF16)
        gate = (jnp.dot(x0, wg[:HALF], preferred_element_type=F32)
                + jnp.dot(x1, wg[HALF:], preferred_element_type=F32) + bg_ref[...])
        up = (jnp.dot(x0, wu[:HALF], preferred_element_type=F32)
              + jnp.dot(x1, wu[HALF:], preferred_element_type=F32) + bu_ref[...])
        gate = jnp.minimum(gate, SWIGLU_LIMIT)
        up = jnp.clip(up, -SWIGLU_LIMIT, SWIGLU_LIMIT)
        act = (up + 1.0) * gate * jax.nn.sigmoid(SWIGLU_ALPHA * gate)
        acc_ref[rows, :] += jnp.dot(act.astype(BF16), wd_ref[...].astype(BF16), preferred_element_type=F32)

    def finish(rows, m):
        y = acc_ref[rows, :] + bd_ref[...]
        o_ref[rows, :] = _pack_pair(y[:, :HALF], y[:, HALF:])

    @pl.when(j == 0)
    def _():
        for_rows(unpack)

    for_rows(expert_mlp)

    @pl.when(j == nj - 1)
    def _():
        for_rows(finish)

        def blank(s, c):
            o_ref[rows_of(s), :] = jnp.zeros((MOE_SUB, HALF), U32)
            return c
        lax.fori_loop(nsub, n_full, blank, 0)


def _moe(block_e, block_nsub, n_active, xs, w_gu, b_gu, w_down, b_down):
    n_rows = xs.shape[0]
    nfc = D_FF // MOE_FC

    def frozen_j(b, j, na):
        return jnp.where(b < na[0], j, nfc - 1)

    grid_spec = pltpu.PrefetchScalarGridSpec(
        num_scalar_prefetch=3,
        grid=(n_rows // MOE_BM, nfc),
        in_specs=[
            pl.BlockSpec((MOE_BM, HALF), lambda b, j, be, ns, na: (jnp.minimum(b, na[0] - 1), 0)),
            pl.BlockSpec((None, D_MODEL, MOE_FC), lambda b, j, be, ns, na: (be[b], 0, frozen_j(b, j, na))),
            pl.BlockSpec((None, D_MODEL, MOE_FC), lambda b, j, be, ns, na: (be[b], 0, nfc + frozen_j(b, j, na))),
            pl.BlockSpec((None, 1, MOE_FC), lambda b, j, be, ns, na: (be[b], 0, frozen_j(b, j, na))),
            pl.BlockSpec((None, 1, MOE_FC), lambda b, j, be, ns, na: (be[b], 0, nfc + frozen_j(b, j, na))),
            pl.BlockSpec((None, MOE_FC, D_MODEL), lambda b, j, be, ns, na: (be[b], frozen_j(b, j, na), 0)),
            pl.BlockSpec((None, 1, D_MODEL), lambda b, j, be, ns, na: (be[b], 0, 0)),
        ],
        out_specs=pl.BlockSpec((MOE_BM, HALF), lambda b, j, be, ns, na: (b, 0)),
        scratch_shapes=[pltpu.VMEM((MOE_BM, D_MODEL), F32), pltpu.VMEM((2, MOE_BM, HALF), BF16)],
    )
    return pl.pallas_call(
        _moe_kernel,
        grid_spec=grid_spec,
        out_shape=jax.ShapeDtypeStruct((n_rows, HALF), U32),
        compiler_params=_params(("arbitrary", "arbitrary")),
        name="moe_experts",
    )(block_e, block_nsub, n_active, xs, w_gu, w_gu, b_gu, b_gu, w_down, b_down)


def _combine_ple_kernel(dcur_ref, dnext_ref, x_ref, wt_ref, ys_ref, g_ref, pp_ref, ps_ref, wg_ref, wp_ref,
                        op_ref, os_ref, buf, sems, *, n_prompt_blocks):
    i = pl.program_id(0)
    n = pl.num_programs(0)
    slot = i % 2

    def fetch(dref, s):
        def body(r, c):
            for k in range(TOP_K):
                d = dref[0, 0, k * CMB_TB + r]
                pltpu.make_async_copy(ys_ref.at[pl.ds(d, 1), :], buf.at[s, k, pl.ds(r, 1), :],
                                      sems.at[s]).start(priority=k % 2)
            return c
        lax.fori_loop(0, CMB_TB, body, 0)

    @pl.when(i == 0)
    def _():
        fetch(dcur_ref, 0)

    @pl.when(i + 1 < n)
    def _():
        fetch(dnext_ref, 1 - slot)

    _row_block_wait(ys_ref.at[pl.ds(0, CMB_TB), :], buf.at[slot, 0], sems.at[slot], TOP_K)

    x_lo = x_ref[:, :HALF]
    x_hi = x_ref[:, HALF:]
    for k in range(TOP_K):
        lo, hi = _unpack_pair(buf[slot, k])
        w = wt_ref[:, k:k + 1]
        x_lo = x_lo + w * lo
        x_hi = x_hi + w * hi

    ms = (jnp.sum(x_lo * x_lo, axis=-1, keepdims=True) + jnp.sum(x_hi * x_hi, axis=-1, keepdims=True)) / D_MODEL
    r = lax.rsqrt(ms + EPS)
    h_lo = (x_lo * r * g_ref[:, :HALF]).astype(BF16)
    h_hi = (x_hi * r * g_ref[:, HALF:]).astype(BF16)
    gate = jax.nn.sigmoid(jnp.dot(h_lo, wg_ref[:HALF, :], preferred_element_type=F32)
                          + jnp.dot(h_hi, wg_ref[HALF:, :], preferred_element_type=F32))

    def finish(p_ref, o_ref):
        pp = jnp.dot(p_ref[...].astype(BF16), wp_ref[...], preferred_element_type=F32)
        o_ref[:, :HALF] = x_lo + gate[:, :HALF] * pp[:, :HALF]
        o_ref[:, HALF:] = x_hi + gate[:, HALF:] * pp[:, HALF:]

    @pl.when(i < n_prompt_blocks)
    def _():
        finish(pp_ref, op_ref)

    @pl.when(i >= n_prompt_blocks)
    def _():
        finish(ps_ref, os_ref)


def _combine_ple(dest_blocks, x1, wgt_t, ys, g_ple, p_p, p_s, w_gate, w_proj):
    t = x1.shape[0]
    nb = t // CMB_TB
    npb = p_p.shape[0] // CMB_TB
    const = lambda i: (0, 0)
    prompt_rows = lambda i: (jnp.minimum(i, npb - 1), 0)
    sample_rows = lambda i: (jnp.maximum(i - npb, 0), 0)
    return pl.pallas_call(
        functools.partial(_combine_ple_kernel, n_prompt_blocks=npb),
        grid=(nb,),
        in_specs=[
            pl.BlockSpec((1, 1, TOP_K * CMB_TB), lambda i: (i, 0, 0), memory_space=pltpu.SMEM),
            pl.BlockSpec((1, 1, TOP_K * CMB_TB), lambda i: (jnp.minimum(i + 1, nb - 1), 0, 0),
                         memory_space=pltpu.SMEM),
            pl.BlockSpec((CMB_TB, D_MODEL), lambda i: (i, 0)),
            pl.BlockSpec((CMB_TB, TOP_K), lambda i: (i, 0)),
            pl.BlockSpec(memory_space=pl.ANY),
            pl.BlockSpec((1, D_MODEL), const),
            pl.BlockSpec((CMB_TB, PLE_DIM), prompt_rows),
            pl.BlockSpec((CMB_TB, PLE_DIM), sample_rows),
            pl.BlockSpec((D_MODEL, D_MODEL), const),
            pl.BlockSpec((PLE_DIM, D_MODEL), const),
        ],
        out_specs=[
            pl.BlockSpec((CMB_TB, D_MODEL), prompt_rows),
            pl.BlockSpec((CMB_TB, D_MODEL), sample_rows),
        ],
        out_shape=[
            jax.ShapeDtypeStruct((p_p.shape[0], D_MODEL), F32),
            jax.ShapeDtypeStruct((p_s.shape[0], D_MODEL), F32),
        ],
        scratch_shapes=[
            pltpu.VMEM((2, TOP_K, CMB_TB, HALF), U32),
            pltpu.SemaphoreType.DMA((2,)),
        ],
        compiler_params=_params(("arbitrary",)),
        name="moe_combine_ple",
    )(dest_blocks, dest_blocks, x1, wgt_t, ys, g_ple, p_p, p_s, w_gate, w_proj)


def _rope_tables(max_len):
    half = HEAD_DIM // 2
    inv_freq = ROPE_THETA ** (-jnp.arange(half, dtype=F32) / half)
    ang = jnp.arange(max_len, dtype=F32)[:, None] * inv_freq[None, :]
    cos, sin = jnp.cos(ang), jnp.sin(ang)
    reps = IN_NC // HEAD_DIM
    return jnp.concatenate([cos, cos] * reps, axis=-1), jnp.concatenate([-sin, sin] * reps, axis=-1)


def _head_matrices():
    col = np.arange(IN_NC)
    same_head = (col[:, None] // HEAD_DIM) == (col[None, :] // HEAD_DIM)
    head_sum = np.where(same_head, 1.0 / HEAD_DIM, 0.0)
    src = (col // HEAD_DIM) * HEAD_DIM + (col % HEAD_DIM + HEAD_DIM // 2) % HEAD_DIM
    head_rot = (col[:, None] == src[None, :]).astype(np.float32)
    return jnp.asarray(head_sum, BF16), jnp.asarray(head_rot, BF16)


def _seq_block_tables(seq_lens):
    pos, first, last = [], [], []
    for s in seq_lens:
        pos += list(range(s // IN_TM))
        nb = s // ATT_QB
        first += [1] + [0] * (nb - 1)
        last += [0] * (nb - 1) + [1]
    return jnp.array(pos, jnp.int32), jnp.array(first, jnp.int32), jnp.array(last, jnp.int32)


def _layer(x_p, x_s, p_p, p_s, seq_lens, g_mix, w_in, q_norm, k_norm, sinks, g_sg_v, w_s, b_s, w_branch, w_o,
           g_ffn, w_router, b_router, w_gu, b_gu, w_down, b_down, g_ple, w_ple_gate, w_ple_proj):
    t = x_p.shape[0] + x_s.shape[0]
    q_end = Q_COLS
    kv_end = Q_COLS + 2 * KV_COLS
    w_in_r = jnp.concatenate([w_in[:, :q_end], w_in[:, kv_end:], w_in[:, q_end:kv_end]], axis=1).astype(BF16)
    cos_t, sin_t = _rope_tables(max(seq_lens))
    head_sum, head_rot = _head_matrices()
    pos_blk, first, last = _seq_block_tables(seq_lens)
    reps = IN_NC // HEAD_DIM
    q_gain = jnp.tile(q_norm * (HEAD_DIM ** -0.5 * LOG2E), reps)[None, :]
    k_gain = jnp.tile(k_norm, reps)[None, :]

    proj = _in_proj(pos_blk, x_p, x_s, g_mix[None, :], w_in_r, q_gain, k_gain, cos_t, sin_t, head_sum, head_rot)
    attn = _attention(proj, sinks * LOG2E, first, last)
    bs_b = jnp.broadcast_to(b_s[:, :, None], (SG_GROUPS, SG_CHUNK, SG_CHUNK))
    merged = _mix(attn, proj, w_branch[0].astype(BF16), w_branch[1].astype(BF16), w_s.astype(BF16), bs_b,
                  g_sg_v[None, :])
    x1 = _out_proj(merged, w_o.astype(BF16), x_p, x_s)

    idx, wgt, rank, cnt = _router(x1, g_ffn[None, :], w_router.T, b_router[:, None])

    counts = cnt[:, 0].astype(jnp.int32)
    padded = (counts + MOE_BM - 1) // MOE_BM * MOE_BM
    padded_end = jnp.cumsum(padded)
    padded_start = padded_end - padded
    experts = jnp.arange(N_EXPERTS, dtype=jnp.int32)
    dest = rank + jnp.sum(jnp.where(idx[:, :, None] == experts, padded_start, 0), axis=-1)
    n_blocks = (t * TOP_K) // MOE_BM + N_EXPERTS
    block_row = jnp.arange(n_blocks, dtype=jnp.int32) * MOE_BM
    n_active = padded_end[-1] // MOE_BM
    block_e = jnp.minimum(jnp.sum(padded_end[None, :] <= block_row[:, None], axis=1), N_EXPERTS - 1)
    block_valid = jnp.clip(padded_start[block_e] + counts[block_e] - block_row, 0, MOE_BM)
    block_nsub = ((block_valid + MOE_SUB - 1) // MOE_SUB).astype(jnp.int32)
    block_e = jnp.where(block_row < padded_end[-1], block_e, block_e[n_active - 1]).astype(jnp.int32)

    def token_blocks(tb):
        return dest.reshape(TOP_K, t // tb, tb).transpose(1, 0, 2).reshape(t // tb, 1, TOP_K * tb)

    xs = _dispatch(padded_start + counts, (-counts) % MOE_SUB, token_blocks(DSP_TB), x1, g_ffn[None, :],
                   n_blocks * MOE_BM)
    ys = _moe(block_e, block_nsub, n_active[None].astype(jnp.int32), xs, w_gu, b_gu[:, None, :], w_down,
              b_down[:, None, :])
    return _combine_ple(token_blocks(CMB_TB), x1, wgt.T, ys, g_ple[None, :], p_p, p_s,
                        w_ple_gate.astype(BF16), w_ple_proj.astype(BF16))


def kernel(x_prompt, x_sample, p_prompt, p_sample, g_mix, w_in, q_norm, k_norm, sinks, g_sg_v, w_s, b_s,
           w_branch, w_o, g_ffn, w_router, b_router, w_gu, b_gu, w_down, b_down, g_ple, w_ple_gate,
           w_ple_proj):
    depth = g_mix.shape[0]
    bp, sp, d = x_prompt.shape
    bs, ss, _ = x_sample.shape
    seq_lens = [sp] * bp + [ss] * bs
    x_p = x_prompt.reshape(bp * sp, d)
    x_s = x_sample.reshape(bs * ss, d)
    for l in range(depth):
        x_p, x_s = _layer(x_p, x_s, p_prompt[l].reshape(bp * sp, PLE_DIM), p_sample[l].reshape(bs * ss, PLE_DIM),
                          seq_lens, g_mix[l], w_in[l], q_norm[l], k_norm[l], sinks[l], g_sg_v[l], w_s[l], b_s[l],
                          w_branch[l], w_o[l], g_ffn[l], w_router[l], b_router[l], w_gu[l], b_gu[l], w_down[l],
                          b_down[l], g_ple[l], w_ple_gate[l], w_ple_proj[l])
    return x_p.reshape(bp, sp, d), x_s.reshape(bs, ss, d)
```

```python
import functools
import math

import jax
import jax.numpy as jnp
import numpy as np
from jax import lax
from jax.experimental import pallas as pl
from jax.experimental.pallas import tpu as pltpu

F32 = jnp.float32
BF16 = jnp.bfloat16
U32 = jnp.uint32

D_MODEL = 2048
HALF = D_MODEL // 2
HEAD_DIM = 128
N_HEADS = 16
N_KV_HEADS = 4
GROUP = N_HEADS // N_KV_HEADS
WINDOW = 128
ROPE_THETA = 10000.0
SG_GROUPS = 16
SG_WIDTH = 2048
SG_CHUNK = 128
N_EXPERTS = 32
TOP_K = 4
D_FF = 2048
SWIGLU_LIMIT = 7.0
SWIGLU_ALPHA = 1.702
PLE_DIM = 256
EPS = 1e-6
NEG_INF = -1e30
LOG2E = math.log2(math.e)

Q_COLS = N_HEADS * HEAD_DIM
KV_COLS = N_KV_HEADS * HEAD_DIM
IN_COLS = Q_COLS + 2 * KV_COLS + 2 * SG_WIDTH + 2 * D_MODEL

COL_Q = 0
COL_U = Q_COLS
COL_SV = COL_U + SG_WIDTH
COL_GA = COL_SV + SG_WIDTH
COL_GS = COL_GA + D_MODEL
COL_K = COL_GS + D_MODEL
COL_V = COL_K + KV_COLS

VMEM_LIMIT = 56 * 1024 * 1024

IN_TM, IN_TN, IN_NC = 1024, 1024, 256
ATT_QB = 1024
MIX_TM, MIX_TN = 512, 1024
OUT_TM, OUT_TN = 1024, 512
RT_TM = 512
MOE_BM, MOE_FC, MOE_SUB = 1024, 256, 256
DSP_TB = 256
CMB_TB = 256
PLE_CHUNKS = 8


def _params(sem):
    return pltpu.CompilerParams(dimension_semantics=sem, vmem_limit_bytes=VMEM_LIMIT)


def _pack_pair(lo, hi):
    def rounded_bits(x):
        return lax.bitcast_convert_type(x.astype(BF16).astype(F32), U32)
    return (rounded_bits(lo) >> 16) | rounded_bits(hi)


def _unpack_pair(u):
    lo = lax.bitcast_convert_type(u << 16, F32)
    hi = lax.bitcast_convert_type(u & U32(0xFFFF0000), F32)
    return lo, hi


def _in_proj_kernel(pos_ref, xp_ref, xs_ref, g_ref, w_ref, qg_ref, kg_ref, cos_ref, sin_ref, hsum_ref, rot_ref,
                    o_ref, hn_ref, *, n_prompt_blocks):
    i = pl.program_id(0)
    j = pl.program_id(1)

    def normalize(x_ref):
        x = x_ref[...]
        ms = jnp.mean(x * x, axis=-1, keepdims=True)
        hn_ref[...] = (x * lax.rsqrt(ms + EPS) * g_ref[...]).astype(BF16)

    @pl.when(jnp.logical_and(j == 0, i < n_prompt_blocks))
    def _():
        normalize(xp_ref)

    @pl.when(jnp.logical_and(j == 0, i >= n_prompt_blocks))
    def _():
        normalize(xs_ref)

    def rope_heads(gain):
        def epilogue(acc, col0):
            ms = jnp.dot((acc * acc).astype(BF16), hsum_ref[...], preferred_element_type=F32)
            an = acc * lax.rsqrt(ms + EPS) * gain
            rot = jnp.dot(an.astype(BF16), rot_ref[...], preferred_element_type=F32)
            o_ref[:, col0:col0 + IN_NC] = (an * cos_ref[...] + rot * sin_ref[...]).astype(BF16)
        return epilogue

    def plain(fn):
        def epilogue(acc, col0):
            o_ref[:, col0:col0 + IN_NC] = fn(acc).astype(BF16)
        return epilogue

    def tiles(epilogues):
        def chunk_dot(c):
            return jnp.dot(hn_ref[...], w_ref[:, c * IN_NC:(c + 1) * IN_NC], preferred_element_type=F32)

        acc = chunk_dot(0)
        for c, epilogue in enumerate(epilogues):
            nxt = chunk_dot(c + 1) if c + 1 < len(epilogues) else None
            epilogue(acc, c * IN_NC)
            acc = nxt

    n_chunks = IN_TN // IN_NC
    jq = COL_U // IN_TN
    ju = COL_GA // IN_TN
    jg = COL_K // IN_TN

    @pl.when(j < jq)
    def _():
        tiles([rope_heads(qg_ref[...])] * n_chunks)

    @pl.when(jnp.logical_and(j >= jq, j < ju))
    def _():
        tiles([plain(jax.nn.gelu)] * n_chunks)

    @pl.when(jnp.logical_and(j >= ju, j < jg))
    def _():
        tiles([plain(jax.nn.sigmoid)] * n_chunks)

    @pl.when(j >= jg)
    def _():
        nk = KV_COLS // IN_NC
        tiles([rope_heads(kg_ref[...])] * nk + [plain(lambda a: a)] * (n_chunks - nk))


def _in_proj(pos_blk, x_p, x_s, g_mix, w_in_r, q_gain, k_gain, cos_t, sin_t, head_sum, head_rot):
    npb = x_p.shape[0] // IN_TM
    t = x_p.shape[0] + x_s.shape[0]
    const = lambda i, j, pb: (0, 0)
    grid_spec = pltpu.PrefetchScalarGridSpec(
        num_scalar_prefetch=1,
        grid=(t // IN_TM, IN_COLS // IN_TN),
        in_specs=[
            pl.BlockSpec((IN_TM, D_MODEL), lambda i, j, pb: (jnp.minimum(i, npb - 1), 0)),
            pl.BlockSpec((IN_TM, D_MODEL), lambda i, j, pb: (jnp.maximum(i - npb, 0), 0)),
            pl.BlockSpec((1, D_MODEL), const),
            pl.BlockSpec((D_MODEL, IN_TN), lambda i, j, pb: (0, j)),
            pl.BlockSpec((1, IN_NC), const),
            pl.BlockSpec((1, IN_NC), const),
            pl.BlockSpec((IN_TM, IN_NC), lambda i, j, pb: (pb[i], 0)),
            pl.BlockSpec((IN_TM, IN_NC), lambda i, j, pb: (pb[i], 0)),
            pl.BlockSpec((IN_NC, IN_NC), const),
            pl.BlockSpec((IN_NC, IN_NC), const),
        ],
        out_specs=pl.BlockSpec((IN_TM, IN_TN), lambda i, j, pb: (i, j)),
        scratch_shapes=[pltpu.VMEM((IN_TM, D_MODEL), BF16)],
    )
    return pl.pallas_call(
        functools.partial(_in_proj_kernel, n_prompt_blocks=npb),
        grid_spec=grid_spec,
        out_shape=jax.ShapeDtypeStruct((t, IN_COLS), BF16),
        compiler_params=_params(("parallel", "arbitrary")),
        name="in_proj",
    )(pos_blk, x_p, x_s, g_mix, w_in_r, q_gain, k_gain, cos_t, sin_t, head_sum, head_rot)


def _attn_kernel(first_ref, last_ref, sinks_ref, q_ref, kp_ref, km_ref, kn_ref,
                 vp_ref, vm_ref, vn_ref, o_ref):
    i = pl.program_id(0)
    kv = pl.program_id(1)
    nqb = ATT_QB // WINDOW
    rows = GROUP * WINDOW
    kband = jnp.concatenate([kp_ref[...], km_ref[...], kn_ref[...]], axis=0)
    vband = jnp.concatenate([vp_ref[...], vm_ref[...], vn_ref[...]], axis=0)
    r = lax.broadcasted_iota(jnp.int32, (rows, 3 * WINDOW), 0) & (WINDOW - 1)
    c = lax.broadcasted_iota(jnp.int32, (rows, 3 * WINDOW), 1)
    d = c - r
    band_ok = jnp.logical_and(d >= 0, d <= 2 * WINDOW)
    lo = jnp.where(first_ref[i] != 0, WINDOW, 0)
    hi = jnp.where(last_ref[i] != 0, 2 * WINDOW, 3 * WINDOW)
    head = lax.broadcasted_iota(jnp.int32, (rows, 1), 0) >> (WINDOW.bit_length() - 1)
    sink = jnp.zeros((rows, 1), F32)
    for g in range(GROUP):
        sink = jnp.where(head == g, sinks_ref[kv * GROUP + g], sink)
    def scores(b):
        kb = kband[b * WINDOW:(b + 3) * WINDOW]
        qrows = slice(b * WINDOW, (b + 1) * WINDOW)
        qs = jnp.concatenate([q_ref[qrows, g * HEAD_DIM:(g + 1) * HEAD_DIM] for g in range(GROUP)], axis=0)
        return lax.dot_general(qs, kb, (((1,), (1,)), ((), ())), preferred_element_type=F32)

    s_next = scores(0)
    for b in range(nqb):
        ok = band_ok
        if b == 0:
            ok = jnp.logical_and(ok, c >= lo)
        if b == nqb - 1:
            ok = jnp.logical_and(ok, c < hi)
        vb = vband[b * WINDOW:(b + 3) * WINDOW]
        qrows = slice(b * WINDOW, (b + 1) * WINDOW)
        s = s_next
        if b + 1 < nqb:
            s_next = scores(b + 1)
        s = jnp.where(ok, s, NEG_INF)
        m = jnp.maximum(jnp.max(s, axis=-1, keepdims=True), sink)
        p = jnp.exp2(s - m)
        den = jnp.sum(p, axis=-1, keepdims=True) + jnp.exp2(sink - m)
        o = jnp.dot(p.astype(BF16), vb, preferred_element_type=F32) / den
        for g in range(GROUP):
            o_ref[qrows, g * HEAD_DIM:(g + 1) * HEAD_DIM] = o[g * WINDOW:(g + 1) * WINDOW].astype(BF16)


def _attention(proj, sinks, first, last):
    t = proj.shape[0]
    nqb = ATT_QB // WINDOW
    nrb = t // WINDOW
    kcol = COL_K // HEAD_DIM
    vcol = COL_V // HEAD_DIM
    qspec = pl.BlockSpec((ATT_QB, GROUP * HEAD_DIM), lambda i, kv, *_: (i, kv))

    def band_specs(col):
        return [
            pl.BlockSpec((WINDOW, HEAD_DIM), lambda i, kv, *_: (jnp.maximum(i * nqb - 1, 0), col + kv)),
            pl.BlockSpec((ATT_QB, HEAD_DIM), lambda i, kv, *_: (i, col + kv)),
            pl.BlockSpec((WINDOW, HEAD_DIM), lambda i, kv, *_: (jnp.minimum((i + 1) * nqb, nrb - 1), col + kv)),
        ]

    grid_spec = pltpu.PrefetchScalarGridSpec(
        num_scalar_prefetch=3,
        grid=(t // ATT_QB, N_KV_HEADS),
        in_specs=[qspec] + band_specs(kcol) + band_specs(vcol),
        out_specs=pl.BlockSpec((ATT_QB, GROUP * HEAD_DIM), lambda i, kv, *_: (i, kv)),
    )
    return pl.pallas_call(
        _attn_kernel,
        grid_spec=grid_spec,
        out_shape=jax.ShapeDtypeStruct((t, Q_COLS), BF16),
        compiler_params=_params(("parallel", "arbitrary")),
        name="window_attn",
    )(first, last, sinks, proj, proj, proj, proj, proj, proj, proj)


def _mix_kernel(attn_ref, u_ref, sv_ref, ga_ref, gs_ref, wa_ref, wb_ref, ws_ref, bs_ref,
                gsv_ref, o_ref, vn_ref, sg_ref):
    j = pl.program_id(1)

    @pl.when(j == 0)
    def _():
        sv = sv_ref[...].astype(F32)
        ms = jnp.mean(sv * sv, axis=-1, keepdims=True)
        vn_ref[...] = (sv * lax.rsqrt(ms + EPS) * gsv_ref[...]).astype(BF16)

        def chunk(c, carry):
            rows = pl.ds(pl.multiple_of(c * SG_CHUNK, SG_CHUNK), SG_CHUNK)
            for g in range(SG_GROUPS):
                cols = slice(g * SG_CHUNK, (g + 1) * SG_CHUNK)
                mixed = jnp.dot(ws_ref[g], vn_ref[rows, cols], preferred_element_type=F32) + bs_ref[g]
                sg_ref[rows, cols] = (u_ref[rows, cols].astype(F32) * mixed).astype(BF16)
            return carry

        lax.fori_loop(0, MIX_TM // SG_CHUNK, chunk, 0)

    pa = jnp.dot(attn_ref[...], wa_ref[...], preferred_element_type=F32)
    ps = jnp.dot(sg_ref[...], wb_ref[...], preferred_element_type=F32)
    merged = ga_ref[...].astype(F32) * pa + gs_ref[...].astype(F32) * ps
    o_ref[...] = merged.astype(BF16)


def _mix(attn, proj, wa, wb, ws, bs_b, gsv):
    t = attn.shape[0]
    grid = (t // MIX_TM, D_MODEL // MIX_TN)
    ga0 = COL_GA // MIX_TN
    gs0 = COL_GS // MIX_TN
    return pl.pallas_call(
        _mix_kernel,
        grid=grid,
        in_specs=[
            pl.BlockSpec((MIX_TM, Q_COLS), lambda i, j: (i, 0)),
            pl.BlockSpec((MIX_TM, SG_WIDTH), lambda i, j: (i, COL_U // SG_WIDTH)),
            pl.BlockSpec((MIX_TM, SG_WIDTH), lambda i, j: (i, COL_SV // SG_WIDTH)),
            pl.BlockSpec((MIX_TM, MIX_TN), lambda i, j: (i, ga0 + j)),
            pl.BlockSpec((MIX_TM, MIX_TN), lambda i, j: (i, gs0 + j)),
            pl.BlockSpec((Q_COLS, MIX_TN), lambda i, j: (0, j)),
            pl.BlockSpec((SG_WIDTH, MIX_TN), lambda i, j: (0, j)),
            pl.BlockSpec((SG_GROUPS, SG_CHUNK, SG_CHUNK), lambda i, j: (0, 0, 0)),
            pl.BlockSpec((SG_GROUPS, SG_CHUNK, SG_CHUNK), lambda i, j: (0, 0, 0)),
            pl.BlockSpec((1, SG_WIDTH), lambda i, j: (0, 0)),
        ],
        out_specs=pl.BlockSpec((MIX_TM, MIX_TN), lambda i, j: (i, j)),
        out_shape=jax.ShapeDtypeStruct((t, D_MODEL), BF16),
        scratch_shapes=[pltpu.VMEM((MIX_TM, SG_WIDTH), BF16), pltpu.VMEM((MIX_TM, SG_WIDTH), BF16)],
        compiler_params=_params(("parallel", "arbitrary")),
        name="branch_mix",
    )(attn, proj, proj, proj, proj, wa, wb, ws, bs_b, gsv)


def _out_proj_kernel(m_ref, w_ref, xp_ref, xs_ref, o_ref, *, n_prompt_blocks):
    i = pl.program_id(0)
    y = jnp.dot(m_ref[...], w_ref[...], preferred_element_type=F32)

    @pl.when(i < n_prompt_blocks)
    def _():
        o_ref[...] = xp_ref[...] + y

    @pl.when(i >= n_prompt_blocks)
    def _():
        o_ref[...] = xs_ref[...] + y


def _out_proj(merged, w_o, x_p, x_s):
    t = merged.shape[0]
    npb = x_p.shape[0] // OUT_TM
    nj = D_MODEL // OUT_TN
    return pl.pallas_call(
        functools.partial(_out_proj_kernel, n_prompt_blocks=npb),
        grid=(t // OUT_TM, nj),
        in_specs=[
            pl.BlockSpec((OUT_TM, D_MODEL), lambda i, j: (i, 0)),
            pl.BlockSpec((D_MODEL, OUT_TN), lambda i, j: (0, j)),
            pl.BlockSpec((OUT_TM, OUT_TN), lambda i, j: (jnp.minimum(i, npb - 1), jnp.where(i < npb, j, nj - 1))),
            pl.BlockSpec((OUT_TM, OUT_TN), lambda i, j: (jnp.maximum(i - npb, 0), jnp.where(i < npb, 0, j))),
        ],
        out_specs=pl.BlockSpec((OUT_TM, OUT_TN), lambda i, j: (i, j)),
        out_shape=jax.ShapeDtypeStruct((t, D_MODEL), F32),
        compiler_params=_params(("parallel", "arbitrary")),
        name="out_proj",
    )(merged, w_o, x_p, x_s)


def _split_bf16(a):
    hi = a.astype(BF16)
    lo = (a - hi.astype(F32)).astype(BF16)
    return hi, lo


def _router_kernel(x_ref, g_ref, wr_ref, br_ref, idx_ref, wgt_ref, rank_ref, cnt_ref, carry_ref):
    step = pl.program_id(0)

    @pl.when(step == 0)
    def _():
        carry_ref[...] = jnp.zeros_like(carry_ref)

    x = x_ref[...]
    ms = jnp.mean(x * x, axis=-1, keepdims=True)
    h = x * lax.rsqrt(ms + EPS) * g_ref[...]

    h_hi, h_lo = _split_bf16(h)
    w_hi, w_lo = _split_bf16(wr_ref[...])
    dn = (((1,), (1,)), ((), ()))
    logits = (lax.dot_general(w_hi, h_hi, dn, preferred_element_type=F32)
              + lax.dot_general(w_hi, h_lo, dn, preferred_element_type=F32)
              + lax.dot_general(w_lo, h_hi, dn, preferred_element_type=F32))
    logits = logits + br_ref[...]

    e_iota = lax.broadcasted_iota(jnp.int32, (N_EXPERTS, RT_TM), 0)
    vals = logits
    tops, idxs = [], []
    onehot = jnp.zeros((N_EXPERTS, RT_TM), F32)
    for _ in range(TOP_K):
        m = jnp.max(vals, axis=0, keepdims=True)
        idx = jnp.min(jnp.where(vals == m, e_iota, N_EXPERTS), axis=0, keepdims=True)
        sel = e_iota == idx
        onehot = onehot + sel.astype(F32)
        vals = jnp.where(sel, -jnp.inf, vals)
        tops.append(m)
        idxs.append(idx)

    exps = [jnp.exp(v - tops[0]) for v in tops]
    den = exps[0] + exps[1] + exps[2] + exps[3]

    tr = lax.broadcasted_iota(jnp.int32, (RT_TM, RT_TM), 0)
    tc = lax.broadcasted_iota(jnp.int32, (RT_TM, RT_TM), 1)
    upper = jnp.where(tr < tc, 1.0, 0.0).astype(BF16)
    before = jnp.dot(onehot.astype(BF16), upper, preferred_element_type=F32) + carry_ref[:, 0:1]

    for k in range(TOP_K):
        sel = e_iota == idxs[k]
        rank = jnp.sum(jnp.where(sel, before, 0.0), axis=0, keepdims=True)
        idx_ref[k:k + 1, :] = idxs[k]
        wgt_ref[k:k + 1, :] = exps[k] / den
        rank_ref[k:k + 1, :] = rank.astype(jnp.int32)

    carry_ref[...] = carry_ref[...] + jnp.sum(onehot, axis=1, keepdims=True)
    cnt_ref[...] = carry_ref[...]


def _router(x1, g_ffn, w_router_t, b_router):
    t = x1.shape[0]
    return pl.pallas_call(
        _router_kernel,
        grid=(t // RT_TM,),
        in_specs=[
            pl.BlockSpec((RT_TM, D_MODEL), lambda i: (i, 0)),
            pl.BlockSpec((1, D_MODEL), lambda i: (0, 0)),
            pl.BlockSpec((N_EXPERTS, D_MODEL), lambda i: (0, 0)),
            pl.BlockSpec((N_EXPERTS, 1), lambda i: (0, 0)),
        ],
        out_specs=[
            pl.BlockSpec((TOP_K, RT_TM), lambda i: (0, i)),
            pl.BlockSpec((TOP_K, RT_TM), lambda i: (0, i)),
            pl.BlockSpec((TOP_K, RT_TM), lambda i: (0, i)),
            pl.BlockSpec((N_EXPERTS, 128), lambda i: (0, 0)),
        ],
        out_shape=[
            jax.ShapeDtypeStruct((TOP_K, t), jnp.int32),
            jax.ShapeDtypeStruct((TOP_K, t), F32),
            jax.ShapeDtypeStruct((TOP_K, t), jnp.int32),
            jax.ShapeDtypeStruct((N_EXPERTS, 128), F32),
        ],
        scratch_shapes=[pltpu.VMEM((N_EXPERTS, 128), F32)],
        compiler_params=_params(("arbitrary",)),
        name="router",
    )(x1, g_ffn, w_router_t, b_router)


def _row_block_wait(src, dst, sem, copies):
    for _ in range(copies):
        pltpu.make_async_copy(src, dst, sem).wait()


def _dispatch_kernel(padstart_ref, padcnt_ref, dest_ref, x_ref, g_ref, xs_ref, hbuf, zrow, sems, zsem):
    i = pl.program_id(0)
    n = pl.num_programs(0)
    slot = i % 2

    def zero_copy(row):
        return pltpu.make_async_copy(zrow.at[pl.ds(0, 1), :], xs_ref.at[pl.ds(row, 1), :], zsem)

    @pl.when(i == 0)
    def _():
        zrow[...] = jnp.zeros_like(zrow)
        for e in range(N_EXPERTS):
            def start(r, c, e=e):
                zero_copy(padstart_ref[e] + r).start()
                return c
            lax.fori_loop(0, padcnt_ref[e], start, 0)

    x = x_ref[...]
    ms = jnp.mean(x * x, axis=-1, keepdims=True)
    h = x * lax.rsqrt(ms + EPS) * g_ref[...]
    hbuf[slot] = _pack_pair(h[:, :HALF], h[:, HALF:])

    def send(r, c):
        for k in range(TOP_K):
            d = dest_ref[0, 0, k * DSP_TB + r]
            pltpu.make_async_copy(hbuf.at[slot, pl.ds(r, 1), :], xs_ref.at[pl.ds(d, 1), :],
                                  sems.at[slot]).start(priority=k % 2)
        return c

    lax.fori_loop(0, DSP_TB, send, 0)

    def wait_slot(s):
        _row_block_wait(hbuf.at[s], xs_ref.at[pl.ds(0, DSP_TB), :], sems.at[s], TOP_K)

    @pl.when(i > 0)
    def _():
        wait_slot(1 - slot)

    @pl.when(i == n - 1)
    def _():
        wait_slot(slot)
        for e in range(N_EXPERTS):
            def done(r, c):
                zero_copy(0).wait()
                return c
            lax.fori_loop(0, padcnt_ref[e], done, 0)


def _dispatch(padstart, padcnt, dest_blocks, x1, g_ffn, n_rows):
    t = x1.shape[0]
    grid_spec = pltpu.PrefetchScalarGridSpec(
        num_scalar_prefetch=2,
        grid=(t // DSP_TB,),
        in_specs=[
            pl.BlockSpec((1, 1, TOP_K * DSP_TB), lambda i, *_: (i, 0, 0), memory_space=pltpu.SMEM),
            pl.BlockSpec((DSP_TB, D_MODEL), lambda i, *_: (i, 0)),
            pl.BlockSpec((1, D_MODEL), lambda i, *_: (0, 0)),
        ],
        out_specs=pl.BlockSpec(memory_space=pl.ANY),
        scratch_shapes=[
            pltpu.VMEM((2, DSP_TB, HALF), U32),
            pltpu.VMEM((8, HALF), U32),
            pltpu.SemaphoreType.DMA((2,)),
            pltpu.SemaphoreType.DMA(()),
        ],
    )
    return pl.pallas_call(
        _dispatch_kernel,
        grid_spec=grid_spec,
        out_shape=jax.ShapeDtypeStruct((n_rows, HALF), U32),
        compiler_params=_params(("arbitrary",)),
        name="moe_dispatch",
    )(padstart, padcnt, dest_blocks, x1, g_ffn)


def _moe_kernel(be_ref, ns_ref, na_ref, x_ref, wg_ref, wu_ref, bg_ref, bu_ref, wd_ref, bd_ref, o_ref,
                acc_ref, xb_ref):
    b = pl.program_id(0)
    j = pl.program_id(1)
    nj = pl.num_programs(1)
    nsub = ns_ref[b]

    n_full = MOE_BM // MOE_SUB
    full = nsub == n_full
    whole = slice(None)

    def rows_of(s):
        return pl.ds(pl.multiple_of(s * MOE_SUB, MOE_SUB), MOE_SUB)

    def for_rows(fn):
        @pl.when(full)
        def _():
            fn(whole, MOE_BM)

        @pl.when(jnp.logical_not(full))
        def _():
            def body(s, c):
                fn(rows_of(s), MOE_SUB)
                return c
            lax.fori_loop(0, nsub, body, 0)

    def unpack(rows, m):
        lo, hi = _unpack_pair(x_ref[rows, :])
        xb_ref[0, rows, :] = lo.astype(BF16)
        xb_ref[1, rows, :] = hi.astype(BF16)
        acc_ref[rows, :] = jnp.zeros((m, D_MODEL), F32)

    def expert_mlp(rows, m):
        wg = wg_ref[...].astype(BF16)
        wu = wu_ref[...].astype(BF16)
        x0 = xb_ref[0, rows, :]
        x1 = xb_ref[1, rows, :]
        gate = (jnp.dot(x0, wg[:HALF], preferred_element_type=F32)
                + jnp.dot(x1, wg[HALF:], preferred_element_type=F32) + bg_ref[...])
        up = (jnp.dot(x0, wu[:HALF], preferred_element_type=F32)
              + jnp.dot(x1, wu[HALF:], preferred_element_type=F32) + bu_ref[...])
        gate = jnp.minimum(gate, SWIGLU_LIMIT)
        up = jnp.clip(up, -SWIGLU_LIMIT, SWIGLU_LIMIT)
        act = (up + 1.0) * gate * jax.nn.sigmoid(SWIGLU_ALPHA * gate)
        acc_ref[rows, :] += jnp.dot(act.astype(BF16), wd_ref[...].astype(BF16), preferred_element_type=F32)

    def finish(rows, m):
        y = acc_ref[rows, :] + bd_ref[...]
        o_ref[rows, :] = _pack_pair(y[:, :HALF], y[:, HALF:])

    @pl.when(j == 0)
    def _():
        for_rows(unpack)

    for_rows(expert_mlp)

    @pl.when(j == nj - 1)
    def _():
        for_rows(finish)

        def blank(s, c):
            o_ref[rows_of(s), :] = jnp.zeros((MOE_SUB, HALF), U32)
            return c
        lax.fori_loop(nsub, n_full, blank, 0)


def _moe(block_e, block_nsub, n_active, xs, w_gu, b_gu, w_down, b_down):
    n_rows = xs.shape[0]
    nfc = D_FF // MOE_FC

    def frozen_j(b, j, na):
        return jnp.where(b < na[0], j, nfc - 1)

    grid_spec = pltpu.PrefetchScalarGridSpec(
        num_scalar_prefetch=3,
        grid=(n_rows // MOE_BM, nfc),
        in_specs=[
            pl.BlockSpec((MOE_BM, HALF), lambda b, j, be, ns, na: (jnp.minimum(b, na[0] - 1), 0)),
            pl.BlockSpec((None, D_MODEL, MOE_FC), lambda b, j, be, ns, na: (be[b], 0, frozen_j(b, j, na))),
            pl.BlockSpec((None, D_MODEL, MOE_FC), lambda b, j, be, ns, na: (be[b], 0, nfc + frozen_j(b, j, na))),
            pl.BlockSpec((None, 1, MOE_FC), lambda b, j, be, ns, na: (be[b], 0, frozen_j(b, j, na))),
            pl.BlockSpec((None, 1, MOE_FC), lambda b, j, be, ns, na: (be[b], 0, nfc + frozen_j(b, j, na))),
            pl.BlockSpec((None, MOE_FC, D_MODEL), lambda b, j, be, ns, na: (be[b], frozen_j(b, j, na), 0)),
            pl.BlockSpec((None, 1, D_MODEL), lambda b, j, be, ns, na: (be[b], 0, 0)),
        ],
        out_specs=pl.BlockSpec((MOE_BM, HALF), lambda b, j, be, ns, na: (b, 0)),
        scratch_shapes=[pltpu.VMEM((MOE_BM, D_MODEL), F32), pltpu.VMEM((2, MOE_BM, HALF), BF16)],
    )
    return pl.pallas_call(
        _moe_kernel,
        grid_spec=grid_spec,
        out_shape=jax.ShapeDtypeStruct((n_rows, HALF), U32),
        compiler_params=_params(("arbitrary", "arbitrary")),
        name="moe_experts",
    )(block_e, block_nsub, n_active, xs, w_gu, w_gu, b_gu, b_gu, w_down, b_down)


def _combine_ple_kernel(dcur_ref, dnext_ref, x_ref, wt_ref, ys_ref, g_ref, pp_ref, ps_ref, wg_ref, wp_ref,
                        op_ref, os_ref, buf, sems, *, n_prompt_blocks):
    i = pl.program_id(0)
    n = pl.num_programs(0)
    slot = i % 2

    def fetch_row(dref, s, r):
        for k in range(TOP_K):
            d = dref[0, 0, k * CMB_TB + r]
            pltpu.make_async_copy(ys_ref.at[pl.ds(d, 1), :], buf.at[s, k, pl.ds(r, 1), :],
                                  sems.at[s]).start(priority=k % 2)

    def wait_slot(s):
        _row_block_wait(ys_ref.at[pl.ds(0, CMB_TB), :], buf.at[s, 0], sems.at[s], TOP_K)

    @pl.when(i == 0)
    def _():
        def body(r, c):
            fetch_row(dcur_ref, 0, r)
            return c
        lax.fori_loop(0, CMB_TB, body, 0)

    wait_slot(slot)

    groups = iter(np.array_split(np.arange(CMB_TB), PLE_CHUNKS))

    def issue_group():
        for row in next(groups):
            fetch_row(dnext_ref, 1 - slot, int(row))

    x_lo = x_ref[:, :HALF]
    x_hi = x_ref[:, HALF:]
    for k in range(TOP_K):
        lo, hi = _unpack_pair(buf[slot, k])
        w = wt_ref[:, k:k + 1]
        x_lo = x_lo + w * lo
        x_hi = x_hi + w * hi

    ms = (jnp.sum(x_lo * x_lo, axis=-1, keepdims=True) + jnp.sum(x_hi * x_hi, axis=-1, keepdims=True)) / D_MODEL
    r = lax.rsqrt(ms + EPS)
    h_lo = (x_lo * r * g_ref[:, :HALF]).astype(BF16)
    h_hi = (x_hi * r * g_ref[:, HALF:]).astype(BF16)

    cols_per_chunk = D_MODEL // PLE_CHUNKS
    gates = []
    for c in range(PLE_CHUNKS):
        issue_group()
        cols = slice(c * cols_per_chunk, (c + 1) * cols_per_chunk)
        gates.append(jax.nn.sigmoid(jnp.dot(h_lo, wg_ref[:HALF, cols], preferred_element_type=F32)
                                    + jnp.dot(h_hi, wg_ref[HALF:, cols], preferred_element_type=F32)))
    gate = jnp.concatenate(gates, axis=1)

    def finish(p_ref, o_ref):
        pp = jnp.dot(p_ref[...].astype(BF16), wp_ref[...], preferred_element_type=F32)
        o_ref[:, :HALF] = x_lo + gate[:, :HALF] * pp[:, :HALF]
        o_ref[:, HALF:] = x_hi + gate[:, HALF:] * pp[:, HALF:]

    @pl.when(i < n_prompt_blocks)
    def _():
        finish(pp_ref, op_ref)

    @pl.when(i >= n_prompt_blocks)
    def _():
        finish(ps_ref, os_ref)

    @pl.when(i == n - 1)
    def _():
        wait_slot(1 - slot)


def _combine_ple(dest_blocks, x1, wgt_t, ys, g_ple, p_p, p_s, w_gate, w_proj):
    t = x1.shape[0]
    nb = t // CMB_TB
    npb = p_p.shape[0] // CMB_TB
    const = lambda i: (0, 0)
    prompt_rows = lambda i: (jnp.minimum(i, npb - 1), 0)
    sample_rows = lambda i: (jnp.maximum(i - npb, 0), 0)
    return pl.pallas_call(
        functools.partial(_combine_ple_kernel, n_prompt_blocks=npb),
        grid=(nb,),
        in_specs=[
            pl.BlockSpec((1, 1, TOP_K * CMB_TB), lambda i: (i, 0, 0), memory_space=pltpu.SMEM),
            pl.BlockSpec((1, 1, TOP_K * CMB_TB), lambda i: (jnp.minimum(i + 1, nb - 1), 0, 0),
                         memory_space=pltpu.SMEM),
            pl.BlockSpec((CMB_TB, D_MODEL), lambda i: (i, 0)),
            pl.BlockSpec((CMB_TB, TOP_K), lambda i: (i, 0)),
            pl.BlockSpec(memory_space=pl.ANY),
            pl.BlockSpec((1, D_MODEL), const),
            pl.BlockSpec((CMB_TB, PLE_DIM), prompt_rows),
            pl.BlockSpec((CMB_TB, PLE_DIM), sample_rows),
            pl.BlockSpec((D_MODEL, D_MODEL), const),
            pl.BlockSpec((PLE_DIM, D_MODEL), const),
        ],
        out_specs=[
            pl.BlockSpec((CMB_TB, D_MODEL), prompt_rows),
            pl.BlockSpec((CMB_TB, D_MODEL), sample_rows),
        ],
        out_shape=[
            jax.ShapeDtypeStruct((p_p.shape[0], D_MODEL), F32),
            jax.ShapeDtypeStruct((p_s.shape[0], D_MODEL), F32),
        ],
        scratch_shapes=[
            pltpu.VMEM((2, TOP_K, CMB_TB, HALF), U32),
            pltpu.SemaphoreType.DMA((2,)),
        ],
        compiler_params=_params(("arbitrary",)),
        name="moe_combine_ple",
    )(dest_blocks, dest_blocks, x1, wgt_t, ys, g_ple, p_p, p_s, w_gate, w_proj)


def _rope_tables(max_len):
    half = HEAD_DIM // 2
    inv_freq = ROPE_THETA ** (-jnp.arange(half, dtype=F32) / half)
    ang = jnp.arange(max_len, dtype=F32)[:, None] * inv_freq[None, :]
    cos, sin = jnp.cos(ang), jnp.sin(ang)
    reps = IN_NC // HEAD_DIM
    return jnp.concatenate([cos, cos] * reps, axis=-1), jnp.concatenate([-sin, sin] * reps, axis=-1)


def _head_matrices():
    col = np.arange(IN_NC)
    same_head = (col[:, None] // HEAD_DIM) == (col[None, :] // HEAD_DIM)
    head_sum = np.where(same_head, 1.0 / HEAD_DIM, 0.0)
    src = (col // HEAD_DIM) * HEAD_DIM + (col % HEAD_DIM + HEAD_DIM // 2) % HEAD_DIM
    head_rot = (col[:, None] == src[None, :]).astype(np.float32)
    return jnp.asarray(head_sum, BF16), jnp.asarray(head_rot, BF16)


def _seq_block_tables(seq_lens):
    pos, first, last = [], [], []
    for s in seq_lens:
        pos += list(range(s // IN_TM))
        nb = s // ATT_QB
        first += [1] + [0] * (nb - 1)
        last += [0] * (nb - 1) + [1]
    return jnp.array(pos, jnp.int32), jnp.array(first, jnp.int32), jnp.array(last, jnp.int32)


def _layer(x_p, x_s, p_p, p_s, seq_lens, g_mix, w_in, q_norm, k_norm, sinks, g_sg_v, w_s, b_s, w_branch, w_o,
           g_ffn, w_router, b_router, w_gu, b_gu, w_down, b_down, g_ple, w_ple_gate, w_ple_proj):
    t = x_p.shape[0] + x_s.shape[0]
    q_end = Q_COLS
    kv_end = Q_COLS + 2 * KV_COLS
    w_in_r = jnp.concatenate([w_in[:, :q_end], w_in[:, kv_end:], w_in[:, q_end:kv_end]], axis=1).astype(BF16)
    cos_t, sin_t = _rope_tables(max(seq_lens))
    head_sum, head_rot = _head_matrices()
    pos_blk, first, last = _seq_block_tables(seq_lens)
    reps = IN_NC // HEAD_DIM
    q_gain = jnp.tile(q_norm * (HEAD_DIM ** -0.5 * LOG2E), reps)[None, :]
    k_gain = jnp.tile(k_norm, reps)[None, :]

    proj = _in_proj(pos_blk, x_p, x_s, g_mix[None, :], w_in_r, q_gain, k_gain, cos_t, sin_t, head_sum, head_rot)
    attn = _attention(proj, sinks * LOG2E, first, last)
    bs_b = jnp.broadcast_to(b_s[:, :, None], (SG_GROUPS, SG_CHUNK, SG_CHUNK))
    merged = _mix(attn, proj, w_branch[0].astype(BF16), w_branch[1].astype(BF16), w_s.astype(BF16), bs_b,
                  g_sg_v[None, :])
    x1 = _out_proj(merged, w_o.astype(BF16), x_p, x_s)

    idx, wgt, rank, cnt = _router(x1, g_ffn[None, :], w_router.T, b_router[:, None])

    counts = cnt[:, 0].astype(jnp.int32)
    padded = (counts + MOE_BM - 1) // MOE_BM * MOE_BM
    padded_end = jnp.cumsum(padded)
    padded_start = padded_end - padded
    experts = jnp.arange(N_EXPERTS, dtype=jnp.int32)
    dest = rank + jnp.sum(jnp.where(idx[:, :, None] == experts, padded_start, 0), axis=-1)
    n_blocks = (t * TOP_K) // MOE_BM + N_EXPERTS
    block_row = jnp.arange(n_blocks, dtype=jnp.int32) * MOE_BM
    n_active = padded_end[-1] // MOE_BM
    block_e = jnp.minimum(jnp.sum(padded_end[None, :] <= block_row[:, None], axis=1), N_EXPERTS - 1)
    block_valid = jnp.clip(padded_start[block_e] + counts[block_e] - block_row, 0, MOE_BM)
    block_nsub = ((block_valid + MOE_SUB - 1) // MOE_SUB).astype(jnp.int32)
    block_e = jnp.where(block_row < padded_end[-1], block_e, block_e[n_active - 1]).astype(jnp.int32)

    def token_blocks(tb):
        return dest.reshape(TOP_K, t // tb, tb).transpose(1, 0, 2).reshape(t // tb, 1, TOP_K * tb)

    xs = _dispatch(padded_start + counts, (-counts) % MOE_SUB, token_blocks(DSP_TB), x1, g_ffn[None, :],
                   n_blocks * MOE_BM)
    ys = _moe(block_e, block_nsub, n_active[None].astype(jnp.int32), xs, w_gu, b_gu[:, None, :], w_down,
              b_down[:, None, :])
    return _combine_ple(token_blocks(CMB_TB), x1, wgt.T, ys, g_ple[None, :], p_p, p_s,
                        w_ple_gate.astype(BF16), w_ple_proj.astype(BF16))


def kernel(x_prompt, x_sample, p_prompt, p_sample, g_mix, w_in, q_norm, k_norm, sinks, g_sg_v, w_s, b_s,
           w_branch, w_o, g_ffn, w_router, b_router, w_gu, b_gu, w_down, b_down, g_ple, w_ple_gate,
           w_ple_proj):
    depth = g_mix.shape[0]
    bp, sp, d = x_prompt.shape
    bs, ss, _ = x_sample.shape
    seq_lens = [sp] * bp + [ss] * bs
    x_p = x_prompt.reshape(bp * sp, d)
    x_s = x_sample.reshape(bs * ss, d)
    for l in range(depth):
        x_p, x_s = _layer(x_p, x_s, p_prompt[l].reshape(bp * sp, PLE_DIM), p_sample[l].reshape(bs * ss, PLE_DIM),
                          seq_lens, g_mix[l], w_in[l], q_norm[l], k_norm[l], sinks[l], g_sg_v[l], w_s[l], b_s[l],
                          w_branch[l], w_o[l], g_ffn[l], w_router[l], b_router[l], w_gu[l], b_gu[l], w_down[l],
                          b_down[l], g_ple[l], w_ple_gate[l], w_ple_proj[l])
    return x_p.reshape(bp, sp, d), x_s.reshape(bs, ss, d)
```

```python
import functools
import math

import jax
import jax.numpy as jnp
import numpy as np
from jax import lax
from jax.experimental import pallas as pl
from jax.experimental.pallas import tpu as pltpu

F32 = jnp.float32
BF16 = jnp.bfloat16
U32 = jnp.uint32

D_MODEL = 2048
HALF = D_MODEL // 2
HEAD_DIM = 128
N_HEADS = 16
N_KV_HEADS = 4
GROUP = N_HEADS // N_KV_HEADS
WINDOW = 128
ROPE_THETA = 10000.0
SG_GROUPS = 16
SG_WIDTH = 2048
SG_CHUNK = 128
N_EXPERTS = 32
TOP_K = 4
D_FF = 2048
SWIGLU_LIMIT = 7.0
SWIGLU_ALPHA = 1.702
PLE_DIM = 256
EPS = 1e-6
NEG_INF = -1e30
LOG2E = math.log2(math.e)

Q_COLS = N_HEADS * HEAD_DIM
KV_COLS = N_KV_HEADS * HEAD_DIM
IN_COLS = Q_COLS + 2 * KV_COLS + 2 * SG_WIDTH + 2 * D_MODEL

COL_Q = 0
COL_U = Q_COLS
COL_SV = COL_U + SG_WIDTH
COL_GA = COL_SV + SG_WIDTH
COL_GS = COL_GA + D_MODEL
COL_K = COL_GS + D_MODEL
COL_V = COL_K + KV_COLS

VMEM_LIMIT = 56 * 1024 * 1024

IN_TM, IN_TN, IN_NC = 1024, 1024, 256
ATT_QB = 1024
MIX_TM, MIX_TN = 512, 1024
OUT_TM, OUT_TN = 1024, 1024
RT_TM = 512
MOE_BM, MOE_FC, MOE_SUB = 1024, 256, 256
DSP_TB = 256
CMB_TB = 256
PLE_CHUNKS = 8


def _params(sem):
    return pltpu.CompilerParams(dimension_semantics=sem, vmem_limit_bytes=VMEM_LIMIT)


def _pack_pair(lo, hi):
    def rounded_bits(x):
        return lax.bitcast_convert_type(x.astype(BF16).astype(F32), U32)
    return (rounded_bits(lo) >> 16) | rounded_bits(hi)


def _unpack_pair(u):
    lo = lax.bitcast_convert_type(u << 16, F32)
    hi = lax.bitcast_convert_type(u & U32(0xFFFF0000), F32)
    return lo, hi


def _in_proj_kernel(pos_ref, xp_ref, xs_ref, g_ref, w_ref, qg_ref, kg_ref, cos_ref, sin_ref, hsum_ref, rot_ref,
                    o_ref, hn_ref, *, n_prompt_blocks):
    i = pl.program_id(0)
    j = pl.program_id(1)

    def normalize(x_ref):
        x = x_ref[...]
        ms = jnp.mean(x * x, axis=-1, keepdims=True)
        hn_ref[...] = (x * lax.rsqrt(ms + EPS) * g_ref[...]).astype(BF16)

    @pl.when(jnp.logical_and(j == 0, i < n_prompt_blocks))
    def _():
        normalize(xp_ref)

    @pl.when(jnp.logical_and(j == 0, i >= n_prompt_blocks))
    def _():
        normalize(xs_ref)

    def rope_heads(gain):
        def epilogue(acc, col0):
            reps = IN_NC // HEAD_DIM
            cos = jnp.concatenate([cos_ref[...]] * reps, axis=1)
            sin = jnp.concatenate([sin_ref[...]] * reps, axis=1)
            ms = jnp.dot((acc * acc).astype(BF16), hsum_ref[...], preferred_element_type=F32)
            an = acc * lax.rsqrt(ms + EPS) * gain
            rot = jnp.dot(an.astype(BF16), rot_ref[...], preferred_element_type=F32)
            o_ref[:, col0:col0 + IN_NC] = (an * cos + rot * sin).astype(BF16)
        return epilogue

    def plain(fn):
        def epilogue(acc, col0):
            o_ref[:, col0:col0 + IN_NC] = fn(acc).astype(BF16)
        return epilogue

    def tiles(epilogues):
        def chunk_dot(c):
            return jnp.dot(hn_ref[...], w_ref[:, c * IN_NC:(c + 1) * IN_NC], preferred_element_type=F32)

        acc = chunk_dot(0)
        for c, epilogue in enumerate(epilogues):
            nxt = chunk_dot(c + 1) if c + 1 < len(epilogues) else None
            epilogue(acc, c * IN_NC)
            acc = nxt

    n_chunks = IN_TN // IN_NC
    jq = COL_U // IN_TN
    ju = COL_GA // IN_TN
    jg = COL_K // IN_TN

    @pl.when(j < jq)
    def _():
        tiles([rope_heads(qg_ref[...])] * n_chunks)

    @pl.when(jnp.logical_and(j >= jq, j < ju))
    def _():
        tiles([plain(jax.nn.gelu)] * n_chunks)

    @pl.when(jnp.logical_and(j >= ju, j < jg))
    def _():
        tiles([plain(jax.nn.sigmoid)] * n_chunks)

    @pl.when(j >= jg)
    def _():
        nk = KV_COLS // IN_NC
        tiles([rope_heads(kg_ref[...])] * nk + [plain(lambda a: a)] * (n_chunks - nk))


def _in_proj(pos_blk, x_p, x_s, g_mix, w_in_r, q_gain, k_gain, cos_t, sin_t, head_sum, head_rot):
    npb = x_p.shape[0] // IN_TM
    t = x_p.shape[0] + x_s.shape[0]
    const = lambda i, j, pb: (0, 0)
    grid_spec = pltpu.PrefetchScalarGridSpec(
        num_scalar_prefetch=1,
        grid=(t // IN_TM, IN_COLS // IN_TN),
        in_specs=[
            pl.BlockSpec((IN_TM, D_MODEL), lambda i, j, pb: (jnp.minimum(i, npb - 1), 0)),
            pl.BlockSpec((IN_TM, D_MODEL), lambda i, j, pb: (jnp.maximum(i - npb, 0), 0)),
            pl.BlockSpec((1, D_MODEL), const),
            pl.BlockSpec((D_MODEL, IN_TN), lambda i, j, pb: (0, j)),
            pl.BlockSpec((1, IN_NC), const),
            pl.BlockSpec((1, IN_NC), const),
            pl.BlockSpec((IN_TM, HEAD_DIM), lambda i, j, pb: (pb[i], 0)),
            pl.BlockSpec((IN_TM, HEAD_DIM), lambda i, j, pb: (pb[i], 0)),
            pl.BlockSpec((IN_NC, IN_NC), const),
            pl.BlockSpec((IN_NC, IN_NC), const),
        ],
        out_specs=pl.BlockSpec((IN_TM, IN_TN), lambda i, j, pb: (i, j)),
        scratch_shapes=[pltpu.VMEM((IN_TM, D_MODEL), BF16)],
    )
    return pl.pallas_call(
        functools.partial(_in_proj_kernel, n_prompt_blocks=npb),
        grid_spec=grid_spec,
        out_shape=jax.ShapeDtypeStruct((t, IN_COLS), BF16),
        compiler_params=_params(("parallel", "arbitrary")),
        name="in_proj",
    )(pos_blk, x_p, x_s, g_mix, w_in_r, q_gain, k_gain, cos_t, sin_t, head_sum, head_rot)


def _attn_kernel(first_ref, last_ref, sinks_ref, q_ref, kp_ref, km_ref, kn_ref,
                 vp_ref, vm_ref, vn_ref, o_ref):
    i = pl.program_id(0)
    kv = pl.program_id(1)
    nqb = ATT_QB // WINDOW
    rows = GROUP * WINDOW
    kband = jnp.concatenate([kp_ref[...], km_ref[...], kn_ref[...]], axis=0)
    vband = jnp.concatenate([vp_ref[...], vm_ref[...], vn_ref[...]], axis=0)
    r = lax.broadcasted_iota(jnp.int32, (rows, 3 * WINDOW), 0) & (WINDOW - 1)
    c = lax.broadcasted_iota(jnp.int32, (rows, 3 * WINDOW), 1)
    d = c - r
    band_ok = jnp.logical_and(d >= 0, d <= 2 * WINDOW)
    lo = jnp.where(first_ref[i] != 0, WINDOW, 0)
    hi = jnp.where(last_ref[i] != 0, 2 * WINDOW, 3 * WINDOW)
    head = lax.broadcasted_iota(jnp.int32, (rows, 1), 0) >> (WINDOW.bit_length() - 1)
    sink = jnp.zeros((rows, 1), F32)
    for g in range(GROUP):
        sink = jnp.where(head == g, sinks_ref[kv * GROUP + g], sink)
    def scores(b):
        kb = kband[b * WINDOW:(b + 3) * WINDOW]
        qrows = slice(b * WINDOW, (b + 1) * WINDOW)
        qs = jnp.concatenate([q_ref[qrows, g * HEAD_DIM:(g + 1) * HEAD_DIM] for g in range(GROUP)], axis=0)
        return lax.dot_general(qs, kb, (((1,), (1,)), ((), ())), preferred_element_type=F32)

    s_next = scores(0)
    for b in range(nqb):
        ok = band_ok
        if b == 0:
            ok = jnp.logical_and(ok, c >= lo)
        if b == nqb - 1:
            ok = jnp.logical_and(ok, c < hi)
        vb = vband[b * WINDOW:(b + 3) * WINDOW]
        qrows = slice(b * WINDOW, (b + 1) * WINDOW)
        s = s_next
        if b + 1 < nqb:
            s_next = scores(b + 1)
        s = jnp.where(ok, s, NEG_INF)
        m = jnp.maximum(jnp.max(s, axis=-1, keepdims=True), sink)
        p = jnp.exp2(s - m)
        den = jnp.sum(p, axis=-1, keepdims=True) + jnp.exp2(sink - m)
        o = jnp.dot(p.astype(BF16), vb, preferred_element_type=F32) / den
        for g in range(GROUP):
            o_ref[qrows, g * HEAD_DIM:(g + 1) * HEAD_DIM] = o[g * WINDOW:(g + 1) * WINDOW].astype(BF16)


def _attention(proj, sinks, first, last):
    t = proj.shape[0]
    nqb = ATT_QB // WINDOW
    nrb = t // WINDOW
    kcol = COL_K // HEAD_DIM
    vcol = COL_V // HEAD_DIM
    qspec = pl.BlockSpec((ATT_QB, GROUP * HEAD_DIM), lambda i, kv, *_: (i, kv))

    def band_specs(col):
        return [
            pl.BlockSpec((WINDOW, HEAD_DIM), lambda i, kv, *_: (jnp.maximum(i * nqb - 1, 0), col + kv)),
            pl.BlockSpec((ATT_QB, HEAD_DIM), lambda i, kv, *_: (i, col + kv)),
            pl.BlockSpec((WINDOW, HEAD_DIM), lambda i, kv, *_: (jnp.minimum((i + 1) * nqb, nrb - 1), col + kv)),
        ]

    grid_spec = pltpu.PrefetchScalarGridSpec(
        num_scalar_prefetch=3,
        grid=(t // ATT_QB, N_KV_HEADS),
        in_specs=[qspec] + band_specs(kcol) + band_specs(vcol),
        out_specs=pl.BlockSpec((ATT_QB, GROUP * HEAD_DIM), lambda i, kv, *_: (i, kv)),
    )
    return pl.pallas_call(
        _attn_kernel,
        grid_spec=grid_spec,
        out_shape=jax.ShapeDtypeStruct((t, Q_COLS), BF16),
        compiler_params=_params(("parallel", "arbitrary")),
        name="window_attn",
    )(first, last, sinks, proj, proj, proj, proj, proj, proj, proj)


def _mix_kernel(attn_ref, u_ref, sv_ref, ga_ref, gs_ref, wa_ref, wb_ref, ws_ref, bs_ref,
                gsv_ref, o_ref, vn_ref, sg_ref):
    j = pl.program_id(1)

    def spatial_gating():
        sv = sv_ref[...].astype(F32)
        ms = jnp.mean(sv * sv, axis=-1, keepdims=True)
        vn_ref[...] = (sv * lax.rsqrt(ms + EPS) * gsv_ref[...]).astype(BF16)
        for c in range(MIX_TM // SG_CHUNK):
            rows = slice(c * SG_CHUNK, (c + 1) * SG_CHUNK)
            for g in range(SG_GROUPS):
                cols = slice(g * SG_CHUNK, (g + 1) * SG_CHUNK)
                mixed = jnp.dot(ws_ref[g], vn_ref[rows, cols], preferred_element_type=F32) + bs_ref[g]
                sg_ref[rows, cols] = (u_ref[rows, cols].astype(F32) * mixed).astype(BF16)

    def project(first):
        pa = jnp.dot(attn_ref[...], wa_ref[...], preferred_element_type=F32)
        if first:
            spatial_gating()
        ps = jnp.dot(sg_ref[...], wb_ref[...], preferred_element_type=F32)
        merged = ga_ref[...].astype(F32) * pa + gs_ref[...].astype(F32) * ps
        o_ref[...] = merged.astype(BF16)

    @pl.when(j == 0)
    def _():
        project(True)

    @pl.when(j > 0)
    def _():
        project(False)


def _mix(attn, proj, wa, wb, ws, bs_b, gsv):
    t = attn.shape[0]
    grid = (t // MIX_TM, D_MODEL // MIX_TN)
    ga0 = COL_GA // MIX_TN
    gs0 = COL_GS // MIX_TN
    return pl.pallas_call(
        _mix_kernel,
        grid=grid,
        in_specs=[
            pl.BlockSpec((MIX_TM, Q_COLS), lambda i, j: (i, 0)),
            pl.BlockSpec((MIX_TM, SG_WIDTH), lambda i, j: (i, COL_U // SG_WIDTH)),
            pl.BlockSpec((MIX_TM, SG_WIDTH), lambda i, j: (i, COL_SV // SG_WIDTH)),
            pl.BlockSpec((MIX_TM, MIX_TN), lambda i, j: (i, ga0 + j)),
            pl.BlockSpec((MIX_TM, MIX_TN), lambda i, j: (i, gs0 + j)),
            pl.BlockSpec((Q_COLS, MIX_TN), lambda i, j: (0, j)),
            pl.BlockSpec((SG_WIDTH, MIX_TN), lambda i, j: (0, j)),
            pl.BlockSpec((SG_GROUPS, SG_CHUNK, SG_CHUNK), lambda i, j: (0, 0, 0)),
            pl.BlockSpec((SG_GROUPS, SG_CHUNK, SG_CHUNK), lambda i, j: (0, 0, 0)),
            pl.BlockSpec((1, SG_WIDTH), lambda i, j: (0, 0)),
        ],
        out_specs=pl.BlockSpec((MIX_TM, MIX_TN), lambda i, j: (i, j)),
        out_shape=jax.ShapeDtypeStruct((t, D_MODEL), BF16),
        scratch_shapes=[pltpu.VMEM((MIX_TM, SG_WIDTH), BF16), pltpu.VMEM((MIX_TM, SG_WIDTH), BF16)],
        compiler_params=_params(("parallel", "arbitrary")),
        name="branch_mix",
    )(attn, proj, proj, proj, proj, wa, wb, ws, bs_b, gsv)


def _out_proj_kernel(m_ref, w_ref, xp_ref, xs_ref, o_ref, *, n_prompt_blocks):
    i = pl.program_id(0)
    y = jnp.dot(m_ref[...], w_ref[...], preferred_element_type=F32)

    @pl.when(i < n_prompt_blocks)
    def _():
        o_ref[...] = xp_ref[...] + y

    @pl.when(i >= n_prompt_blocks)
    def _():
        o_ref[...] = xs_ref[...] + y


def _out_proj(merged, w_o, x_p, x_s):
    t = merged.shape[0]
    npb = x_p.shape[0] // OUT_TM
    nj = D_MODEL // OUT_TN
    return pl.pallas_call(
        functools.partial(_out_proj_kernel, n_prompt_blocks=npb),
        grid=(t // OUT_TM, nj),
        in_specs=[
            pl.BlockSpec((OUT_TM, D_MODEL), lambda i, j: (i, 0)),
            pl.BlockSpec((D_MODEL, OUT_TN), lambda i, j: (0, j)),
            pl.BlockSpec((OUT_TM, OUT_TN), lambda i, j: (jnp.minimum(i, npb - 1), jnp.where(i < npb, j, nj - 1))),
            pl.BlockSpec((OUT_TM, OUT_TN), lambda i, j: (jnp.maximum(i - npb, 0), jnp.where(i < npb, 0, j))),
        ],
        out_specs=pl.BlockSpec((OUT_TM, OUT_TN), lambda i, j: (i, j)),
        out_shape=jax.ShapeDtypeStruct((t, D_MODEL), F32),
        compiler_params=_params(("parallel", "arbitrary")),
        name="out_proj",
    )(merged, w_o, x_p, x_s)


def _split_bf16(a):
    hi = a.astype(BF16)
    lo = (a - hi.astype(F32)).astype(BF16)
    return hi, lo


def _router_kernel(x_ref, g_ref, wr_ref, br_ref, idx_ref, wgt_ref, rank_ref, cnt_ref, carry_ref):
    step = pl.program_id(0)

    @pl.when(step == 0)
    def _():
        carry_ref[...] = jnp.zeros_like(carry_ref)

    x = x_ref[...]
    ms = jnp.mean(x * x, axis=-1, keepdims=True)
    h = x * lax.rsqrt(ms + EPS) * g_ref[...]

    h_hi, h_lo = _split_bf16(h)
    w_hi, w_lo = _split_bf16(wr_ref[...])
    dn = (((1,), (1,)), ((), ()))
    logits = (lax.dot_general(w_hi, h_hi, dn, preferred_element_type=F32)
              + lax.dot_general(w_hi, h_lo, dn, preferred_element_type=F32)
              + lax.dot_general(w_lo, h_hi, dn, preferred_element_type=F32))
    logits = logits + br_ref[...]

    e_iota = lax.broadcasted_iota(jnp.int32, (N_EXPERTS, RT_TM), 0)
    vals = logits
    tops, idxs = [], []
    onehot = jnp.zeros((N_EXPERTS, RT_TM), F32)
    for _ in range(TOP_K):
        m = jnp.max(vals, axis=0, keepdims=True)
        idx = jnp.min(jnp.where(vals == m, e_iota, N_EXPERTS), axis=0, keepdims=True)
        sel = e_iota == idx
        onehot = onehot + sel.astype(F32)
        vals = jnp.where(sel, -jnp.inf, vals)
        tops.append(m)
        idxs.append(idx)

    exps = [jnp.exp(v - tops[0]) for v in tops]
    den = exps[0] + exps[1] + exps[2] + exps[3]

    tr = lax.broadcasted_iota(jnp.int32, (RT_TM, RT_TM), 0)
    tc = lax.broadcasted_iota(jnp.int32, (RT_TM, RT_TM), 1)
    upper = jnp.where(tr < tc, 1.0, 0.0).astype(BF16)
    before = jnp.dot(onehot.astype(BF16), upper, preferred_element_type=F32) + carry_ref[:, 0:1]

    for k in range(TOP_K):
        sel = e_iota == idxs[k]
        rank = jnp.sum(jnp.where(sel, before, 0.0), axis=0, keepdims=True)
        idx_ref[k:k + 1, :] = idxs[k]
        wgt_ref[k:k + 1, :] = exps[k] / den
        rank_ref[k:k + 1, :] = rank.astype(jnp.int32)

    carry_ref[...] = carry_ref[...] + jnp.sum(onehot, axis=1, keepdims=True)
    cnt_ref[...] = carry_ref[...]


def _router(x1, g_ffn, w_router_t, b_router):
    t = x1.shape[0]
    return pl.pallas_call(
        _router_kernel,
        grid=(t // RT_TM,),
        in_specs=[
            pl.BlockSpec((RT_TM, D_MODEL), lambda i: (i, 0)),
            pl.BlockSpec((1, D_MODEL), lambda i: (0, 0)),
            pl.BlockSpec((N_EXPERTS, D_MODEL), lambda i: (0, 0)),
            pl.BlockSpec((N_EXPERTS, 1), lambda i: (0, 0)),
        ],
        out_specs=[
            pl.BlockSpec((TOP_K, RT_TM), lambda i: (0, i)),
            pl.BlockSpec((TOP_K, RT_TM), lambda i: (0, i)),
            pl.BlockSpec((TOP_K, RT_TM), lambda i: (0, i)),
            pl.BlockSpec((N_EXPERTS, 128), lambda i: (0, 0)),
        ],
        out_shape=[
            jax.ShapeDtypeStruct((TOP_K, t), jnp.int32),
            jax.ShapeDtypeStruct((TOP_K, t), F32),
            jax.ShapeDtypeStruct((TOP_K, t), jnp.int32),
            jax.ShapeDtypeStruct((N_EXPERTS, 128), F32),
        ],
        scratch_shapes=[pltpu.VMEM((N_EXPERTS, 128), F32)],
        compiler_params=_params(("arbitrary",)),
        name="router",
    )(x1, g_ffn, w_router_t, b_router)


def _row_block_wait(src, dst, sem, copies):
    for _ in range(copies):
        pltpu.make_async_copy(src, dst, sem).wait()


def _dispatch_kernel(padstart_ref, padcnt_ref, dest_ref, x_ref, g_ref, xs_ref, hbuf, zrow, sems, zsem):
    i = pl.program_id(0)
    n = pl.num_programs(0)
    slot = i % 2

    def zero_copy(row):
        return pltpu.make_async_copy(zrow.at[pl.ds(0, 1), :], xs_ref.at[pl.ds(row, 1), :], zsem)

    @pl.when(i == 0)
    def _():
        zrow[...] = jnp.zeros_like(zrow)
        for e in range(N_EXPERTS):
            def start(r, c, e=e):
                zero_copy(padstart_ref[e] + r).start()
                return c
            lax.fori_loop(0, padcnt_ref[e], start, 0)

    x = x_ref[...]
    ms = jnp.mean(x * x, axis=-1, keepdims=True)
    h = x * lax.rsqrt(ms + EPS) * g_ref[...]
    hbuf[slot] = _pack_pair(h[:, :HALF], h[:, HALF:])

    def send(r, c):
        for k in range(TOP_K):
            d = dest_ref[0, 0, k * DSP_TB + r]
            pltpu.make_async_copy(hbuf.at[slot, pl.ds(r, 1), :], xs_ref.at[pl.ds(d, 1), :],
                                  sems.at[slot]).start(priority=k % 2)
        return c

    lax.fori_loop(0, DSP_TB, send, 0)

    def wait_slot(s):
        _row_block_wait(hbuf.at[s], xs_ref.at[pl.ds(0, DSP_TB), :], sems.at[s], TOP_K)

    @pl.when(i > 0)
    def _():
        wait_slot(1 - slot)

    @pl.when(i == n - 1)
    def _():
        wait_slot(slot)
        for e in range(N_EXPERTS):
            def done(r, c):
                zero_copy(0).wait()
                return c
            lax.fori_loop(0, padcnt_ref[e], done, 0)


def _dispatch(padstart, padcnt, dest_blocks, x1, g_ffn, n_rows):
    t = x1.shape[0]
    grid_spec = pltpu.PrefetchScalarGridSpec(
        num_scalar_prefetch=2,
        grid=(t // DSP_TB,),
        in_specs=[
            pl.BlockSpec((1, 1, TOP_K * DSP_TB), lambda i, *_: (i, 0, 0), memory_space=pltpu.SMEM),
            pl.BlockSpec((DSP_TB, D_MODEL), lambda i, *_: (i, 0)),
            pl.BlockSpec((1, D_MODEL), lambda i, *_: (0, 0)),
        ],
        out_specs=pl.BlockSpec(memory_space=pl.ANY),
        scratch_shapes=[
            pltpu.VMEM((2, DSP_TB, HALF), U32),
            pltpu.VMEM((8, HALF), U32),
            pltpu.SemaphoreType.DMA((2,)),
            pltpu.SemaphoreType.DMA(()),
        ],
    )
    return pl.pallas_call(
        _dispatch_kernel,
        grid_spec=grid_spec,
        out_shape=jax.ShapeDtypeStruct((n_rows, HALF), U32),
        compiler_params=_params(("arbitrary",)),
        name="moe_dispatch",
    )(padstart, padcnt, dest_blocks, x1, g_ffn)


def _moe_kernel(be_ref, ns_ref, na_ref, x_ref, wg_ref, wu_ref, bg_ref, bu_ref, wd_ref, bd_ref, o_ref,
                acc_ref, xb_ref):
    b = pl.program_id(0)
    j = pl.program_id(1)
    nj = pl.num_programs(1)
    nsub = ns_ref[b]

    n_full = MOE_BM // MOE_SUB
    full = nsub == n_full
    whole = slice(None)

    def rows_of(s):
        return pl.ds(pl.multiple_of(s * MOE_SUB, MOE_SUB), MOE_SUB)

    def for_rows(fn):
        @pl.when(full)
        def _():
            fn(whole, MOE_BM)

        @pl.when(jnp.logical_not(full))
        def _():
            def body(s, c):
                fn(rows_of(s), MOE_SUB)
                return c
            lax.fori_loop(0, nsub, body, 0)

    def unpack(rows, m):
        lo, hi = _unpack_pair(x_ref[rows, :])
        xb_ref[0, rows, :] = lo.astype(BF16)
        xb_ref[1, rows, :] = hi.astype(BF16)
        acc_ref[rows, :] = jnp.zeros((m, D_MODEL), F32)

    def expert_mlp(rows, m):
        wg = wg_ref[...].astype(BF16)
        wu = wu_ref[...].astype(BF16)
        x0 = xb_ref[0, rows, :]
        x1 = xb_ref[1, rows, :]
        gate = (jnp.dot(x0, wg[:HALF], preferred_element_type=F32)
                + jnp.dot(x1, wg[HALF:], preferred_element_type=F32) + bg_ref[...])
        up = (jnp.dot(x0, wu[:HALF], preferred_element_type=F32)
              + jnp.dot(x1, wu[HALF:], preferred_element_type=F32) + bu_ref[...])
        gate = jnp.minimum(gate, SWIGLU_LIMIT)
        up = jnp.clip(up, -SWIGLU_LIMIT, SWIGLU_LIMIT)
        act = (up + 1.0) * gate * jax.nn.sigmoid(SWIGLU_ALPHA * gate)
        acc_ref[rows, :] += jnp.dot(act.astype(BF16), wd_ref[...].astype(BF16), preferred_element_type=F32)

    def finish(rows, m):
        y = acc_ref[rows, :] + bd_ref[...]
        o_ref[rows, :] = _pack_pair(y[:, :HALF], y[:, HALF:])

    @pl.when(j == 0)
    def _():
        for_rows(unpack)

    for_rows(expert_mlp)

    @pl.when(j == nj - 1)
    def _():
        for_rows(finish)

        def blank(s, c):
            o_ref[rows_of(s), :] = jnp.zeros((MOE_SUB, HALF), U32)
            return c
        lax.fori_loop(nsub, n_full, blank, 0)


def _moe(block_e, block_nsub, n_active, xs, w_gu, b_gu, w_down, b_down):
    n_rows = xs.shape[0]
    nfc = D_FF // MOE_FC

    def frozen_j(b, j, na):
        return jnp.where(b < na[0], j, nfc - 1)

    grid_spec = pltpu.PrefetchScalarGridSpec(
        num_scalar_prefetch=3,
        grid=(n_rows // MOE_BM, nfc),
        in_specs=[
            pl.BlockSpec((MOE_BM, HALF), lambda b, j, be, ns, na: (jnp.minimum(b, na[0] - 1), 0)),
            pl.BlockSpec((None, D_MODEL, MOE_FC), lambda b, j, be, ns, na: (be[b], 0, frozen_j(b, j, na))),
            pl.BlockSpec((None, D_MODEL, MOE_FC), lambda b, j, be, ns, na: (be[b], 0, nfc + frozen_j(b, j, na))),
            pl.BlockSpec((None, 1, MOE_FC), lambda b, j, be, ns, na: (be[b], 0, frozen_j(b, j, na))),
            pl.BlockSpec((None, 1, MOE_FC), lambda b, j, be, ns, na: (be[b], 0, nfc + frozen_j(b, j, na))),
            pl.BlockSpec((None, MOE_FC, D_MODEL), lambda b, j, be, ns, na: (be[b], frozen_j(b, j, na), 0)),
            pl.BlockSpec((None, 1, D_MODEL), lambda b, j, be, ns, na: (be[b], 0, 0)),
        ],
        out_specs=pl.BlockSpec((MOE_BM, HALF), lambda b, j, be, ns, na: (b, 0)),
        scratch_shapes=[pltpu.VMEM((MOE_BM, D_MODEL), F32), pltpu.VMEM((2, MOE_BM, HALF), BF16)],
    )
    return pl.pallas_call(
        _moe_kernel,
        grid_spec=grid_spec,
        out_shape=jax.ShapeDtypeStruct((n_rows, HALF), U32),
        compiler_params=_params(("arbitrary", "arbitrary")),
        name="moe_experts",
    )(block_e, block_nsub, n_active, xs, w_gu, w_gu, b_gu, b_gu, w_down, b_down)


def _combine_ple_kernel(dcur_ref, dnext_ref, x_ref, wt_ref, ys_ref, g_ref, pp_ref, ps_ref, wg_ref, wp_ref,
                        op_ref, os_ref, buf, sems, *, n_prompt_blocks):
    i = pl.program_id(0)
    n = pl.num_programs(0)
    slot = i % 2

    def fetch_row(dref, s, r):
        for k in range(TOP_K):
            d = dref[0, 0, k * CMB_TB + r]
            pltpu.make_async_copy(ys_ref.at[pl.ds(d, 1), :], buf.at[s, k, pl.ds(r, 1), :],
                                  sems.at[s]).start(priority=k % 2)

    def wait_slot(s):
        _row_block_wait(ys_ref.at[pl.ds(0, CMB_TB), :], buf.at[s, 0], sems.at[s], TOP_K)

    @pl.when(i == 0)
    def _():
        def body(r, c):
            fetch_row(dcur_ref, 0, r)
            return c
        lax.fori_loop(0, CMB_TB, body, 0)

    wait_slot(slot)

    groups = iter(np.array_split(np.arange(CMB_TB), PLE_CHUNKS))

    def issue_group():
        for row in next(groups):
            fetch_row(dnext_ref, 1 - slot, int(row))

    x_lo = x_ref[:, :HALF]
    x_hi = x_ref[:, HALF:]
    for k in range(TOP_K):
        lo, hi = _unpack_pair(buf[slot, k])
        w = wt_ref[:, k:k + 1]
        x_lo = x_lo + w * lo
        x_hi = x_hi + w * hi

    ms = (jnp.sum(x_lo * x_lo, axis=-1, keepdims=True) + jnp.sum(x_hi * x_hi, axis=-1, keepdims=True)) / D_MODEL
    r = lax.rsqrt(ms + EPS)
    h_lo = (x_lo * r * g_ref[:, :HALF]).astype(BF16)
    h_hi = (x_hi * r * g_ref[:, HALF:]).astype(BF16)

    cols_per_chunk = D_MODEL // PLE_CHUNKS
    gates = []
    for c in range(PLE_CHUNKS):
        issue_group()
        cols = slice(c * cols_per_chunk, (c + 1) * cols_per_chunk)
        gates.append(jax.nn.sigmoid(jnp.dot(h_lo, wg_ref[:HALF, cols], preferred_element_type=F32)
                                    + jnp.dot(h_hi, wg_ref[HALF:, cols], preferred_element_type=F32)))
    gate = jnp.concatenate(gates, axis=1)

    def finish(p_ref, o_ref):
        pp = jnp.dot(p_ref[...].astype(BF16), wp_ref[...], preferred_element_type=F32)
        o_ref[:, :HALF] = x_lo + gate[:, :HALF] * pp[:, :HALF]
        o_ref[:, HALF:] = x_hi + gate[:, HALF:] * pp[:, HALF:]

    @pl.when(i < n_prompt_blocks)
    def _():
        finish(pp_ref, op_ref)

    @pl.when(i >= n_prompt_blocks)
    def _():
        finish(ps_ref, os_ref)

    @pl.when(i == n - 1)
    def _():
        wait_slot(1 - slot)


def _combine_ple(dest_blocks, x1, wgt_t, ys, g_ple, p_p, p_s, w_gate, w_proj):
    t = x1.shape[0]
    nb = t // CMB_TB
    npb = p_p.shape[0] // CMB_TB
    const = lambda i: (0, 0)
    prompt_rows = lambda i: (jnp.minimum(i, npb - 1), 0)
    sample_rows = lambda i: (jnp.maximum(i - npb, 0), 0)
    return pl.pallas_call(
        functools.partial(_combine_ple_kernel, n_prompt_blocks=npb),
        grid=(nb,),
        in_specs=[
            pl.BlockSpec((1, 1, TOP_K * CMB_TB), lambda i: (i, 0, 0), memory_space=pltpu.SMEM),
            pl.BlockSpec((1, 1, TOP_K * CMB_TB), lambda i: (jnp.minimum(i + 1, nb - 1), 0, 0),
                         memory_space=pltpu.SMEM),
            pl.BlockSpec((CMB_TB, D_MODEL), lambda i: (i, 0)),
            pl.BlockSpec((CMB_TB, TOP_K), lambda i: (i, 0)),
            pl.BlockSpec(memory_space=pl.ANY),
            pl.BlockSpec((1, D_MODEL), const),
            pl.BlockSpec((CMB_TB, PLE_DIM), prompt_rows),
            pl.BlockSpec((CMB_TB, PLE_DIM), sample_rows),
            pl.BlockSpec((D_MODEL, D_MODEL), const),
            pl.BlockSpec((PLE_DIM, D_MODEL), const),
        ],
        out_specs=[
            pl.BlockSpec((CMB_TB, D_MODEL), prompt_rows),
            pl.BlockSpec((CMB_TB, D_MODEL), sample_rows),
        ],
        out_shape=[
            jax.ShapeDtypeStruct((p_p.shape[0], D_MODEL), F32),
            jax.ShapeDtypeStruct((p_s.shape[0], D_MODEL), F32),
        ],
        scratch_shapes=[
            pltpu.VMEM((2, TOP_K, CMB_TB, HALF), U32),
            pltpu.SemaphoreType.DMA((2,)),
        ],
        compiler_params=_params(("arbitrary",)),
        name="moe_combine_ple",
    )(dest_blocks, dest_blocks, x1, wgt_t, ys, g_ple, p_p, p_s, w_gate, w_proj)


def _rope_tables(max_len):
    half = HEAD_DIM // 2
    inv_freq = ROPE_THETA ** (-jnp.arange(half, dtype=F32) / half)
    ang = jnp.arange(max_len, dtype=F32)[:, None] * inv_freq[None, :]
    cos, sin = jnp.cos(ang), jnp.sin(ang)
    return jnp.concatenate([cos, cos], axis=-1), jnp.concatenate([-sin, sin], axis=-1)


def _head_matrices():
    col = np.arange(IN_NC)
    same_head = (col[:, None] // HEAD_DIM) == (col[None, :] // HEAD_DIM)
    head_sum = np.where(same_head, 1.0 / HEAD_DIM, 0.0)
    src = (col // HEAD_DIM) * HEAD_DIM + (col % HEAD_DIM + HEAD_DIM // 2) % HEAD_DIM
    head_rot = (col[:, None] == src[None, :]).astype(np.float32)
    return jnp.asarray(head_sum, BF16), jnp.asarray(head_rot, BF16)


def _seq_block_tables(seq_lens):
    pos, first, last = [], [], []
    for s in seq_lens:
        pos += list(range(s // IN_TM))
        nb = s // ATT_QB
        first += [1] + [0] * (nb - 1)
        last += [0] * (nb - 1) + [1]
    return jnp.array(pos, jnp.int32), jnp.array(first, jnp.int32), jnp.array(last, jnp.int32)


def _layer(x_p, x_s, p_p, p_s, seq_lens, g_mix, w_in, q_norm, k_norm, sinks, g_sg_v, w_s, b_s, w_branch, w_o,
           g_ffn, w_router, b_router, w_gu, b_gu, w_down, b_down, g_ple, w_ple_gate, w_ple_proj):
    t = x_p.shape[0] + x_s.shape[0]
    q_end = Q_COLS
    kv_end = Q_COLS + 2 * KV_COLS
    w_in_r = jnp.concatenate([w_in[:, :q_end], w_in[:, kv_end:], w_in[:, q_end:kv_end]], axis=1).astype(BF16)
    cos_t, sin_t = _rope_tables(max(seq_lens))
    head_sum, head_rot = _head_matrices()
    pos_blk, first, last = _seq_block_tables(seq_lens)
    reps = IN_NC // HEAD_DIM
    q_gain = jnp.tile(q_norm * (HEAD_DIM ** -0.5 * LOG2E), reps)[None, :]
    k_gain = jnp.tile(k_norm, reps)[None, :]

    proj = _in_proj(pos_blk, x_p, x_s, g_mix[None, :], w_in_r, q_gain, k_gain, cos_t, sin_t, head_sum, head_rot)
    attn = _attention(proj, sinks * LOG2E, first, last)
    bs_b = jnp.broadcast_to(b_s[:, :, None], (SG_GROUPS, SG_CHUNK, SG_CHUNK))
    merged = _mix(attn, proj, w_branch[0].astype(BF16), w_branch[1].astype(BF16), w_s.astype(BF16), bs_b,
                  g_sg_v[None, :])
    x1 = _out_proj(merged, w_o.astype(BF16), x_p, x_s)

    idx, wgt, rank, cnt = _router(x1, g_ffn[None, :], w_router.T, b_router[:, None])

    counts = cnt[:, 0].astype(jnp.int32)
    padded = (counts + MOE_BM - 1) // MOE_BM * MOE_BM
    padded_end = jnp.cumsum(padded)
    padded_start = padded_end - padded
    experts = jnp.arange(N_EXPERTS, dtype=jnp.int32)
    dest = rank + jnp.sum(jnp.where(idx[:, :, None] == experts, padded_start, 0), axis=-1)
    n_blocks = (t * TOP_K) // MOE_BM + N_EXPERTS
    block_row = jnp.arange(n_blocks, dtype=jnp.int32) * MOE_BM
    n_active = padded_end[-1] // MOE_BM
    block_e = jnp.minimum(jnp.sum(padded_end[None, :] <= block_row[:, None], axis=1), N_EXPERTS - 1)
    block_valid = jnp.clip(padded_start[block_e] + counts[block_e] - block_row, 0, MOE_BM)
    block_nsub = ((block_valid + MOE_SUB - 1) // MOE_SUB).astype(jnp.int32)
    block_e = jnp.where(block_row < padded_end[-1], block_e, block_e[n_active - 1]).astype(jnp.int32)

    def token_blocks(tb):
        return dest.reshape(TOP_K, t // tb, tb).transpose(1, 0, 2).reshape(t // tb, 1, TOP_K * tb)

    xs = _dispatch(padded_start + counts, (-counts) % MOE_SUB, token_blocks(DSP_TB), x1, g_ffn[None, :],
                   n_blocks * MOE_BM)
    ys = _moe(block_e, block_nsub, n_active[None].astype(jnp.int32), xs, w_gu, b_gu[:, None, :], w_down,
              b_down[:, None, :])
    return _combine_ple(token_blocks(CMB_TB), x1, wgt.T, ys, g_ple[None, :], p_p, p_s,
                        w_ple_gate.astype(BF16), w_ple_proj.astype(BF16))


def kernel(x_prompt, x_sample, p_prompt, p_sample, g_mix, w_in, q_norm, k_norm, sinks, g_sg_v, w_s, b_s,
           w_branch, w_o, g_ffn, w_router, b_router, w_gu, b_gu, w_down, b_down, g_ple, w_ple_gate,
           w_ple_proj):
    depth = g_mix.shape[0]
    bp, sp, d = x_prompt.shape
    bs, ss, _ = x_sample.shape
    seq_lens = [sp] * bp + [ss] * bs
    x_p = x_prompt.reshape(bp * sp, d)
    x_s = x_sample.reshape(bs * ss, d)
    for l in range(depth):
        x_p, x_s = _layer(x_p, x_s, p_prompt[l].reshape(bp * sp, PLE_DIM), p_sample[l].reshape(bs * ss, PLE_DIM),
                          seq_lens, g_mix[l], w_in[l], q_norm[l], k_norm[l], sinks[l], g_sg_v[l], w_s[l], b_s[l],
                          w_branch[l], w_o[l], g_ffn[l], w_router[l], b_router[l], w_gu[l], b_gu[l], w_down[l],
                          b_down[l], g_ple[l], w_ple_gate[l], w_ple_proj[l])
    return x_p.reshape(bp, sp, d), x_s.reshape(bs, ss, d)
```

```python
import functools
import math

import jax
import jax.numpy as jnp
import numpy as np
from jax import lax
from jax.experimental import pallas as pl
from jax.experimental.pallas import tpu as pltpu

F32 = jnp.float32
BF16 = jnp.bfloat16
U32 = jnp.uint32

D_MODEL = 2048
HALF = D_MODEL // 2
HEAD_DIM = 128
N_HEADS = 16
N_KV_HEADS = 4
GROUP = N_HEADS // N_KV_HEADS
WINDOW = 128
ROPE_THETA = 10000.0
SG_GROUPS = 16
SG_WIDTH = 2048
SG_CHUNK = 128
N_EXPERTS = 32
TOP_K = 4
D_FF = 2048
SWIGLU_LIMIT = 7.0
SWIGLU_ALPHA = 1.702
PLE_DIM = 256
EPS = 1e-6
NEG_INF = -1e30
LOG2E = math.log2(math.e)

Q_COLS = N_HEADS * HEAD_DIM
KV_COLS = N_KV_HEADS * HEAD_DIM
IN_COLS = Q_COLS + 2 * KV_COLS + 2 * SG_WIDTH + 2 * D_MODEL

COL_Q = 0
COL_K = Q_COLS
COL_V = COL_K + KV_COLS
COL_U = COL_V + KV_COLS
COL_SV = COL_U + SG_WIDTH
COL_GA = COL_SV + SG_WIDTH
COL_GS = COL_GA + D_MODEL

VMEM_LIMIT = 56 * 1024 * 1024

IN_TM, IN_TN, IN_NC = 1024, 1024, 256
ATT_QB = 1024
MIX_TM, MIX_TN = 512, 1024
OUT_TM, OUT_TN = 1024, 1024
RT_TM = 512
MOE_BM, MOE_FC, MOE_SUB = 1024, 256, 256
DSP_TB = 256
CMB_TB = 256
PLE_CHUNKS = 8


def _params(sem):
    return pltpu.CompilerParams(dimension_semantics=sem, vmem_limit_bytes=VMEM_LIMIT)


def _pack_pair(lo, hi):
    def rounded_bits(x):
        return lax.bitcast_convert_type(x.astype(BF16).astype(F32), U32)
    return (rounded_bits(lo) >> 16) | rounded_bits(hi)


def _unpack_pair(u):
    lo = lax.bitcast_convert_type(u << 16, F32)
    hi = lax.bitcast_convert_type(u & U32(0xFFFF0000), F32)
    return lo, hi


def _in_proj_kernel(pos_ref, xp_ref, xs_ref, g_ref, w_ref, qg_ref, kg_ref, cos_ref, sin_ref, hsum_ref, rot_ref,
                    o_ref, hn_ref, *, n_prompt_blocks):
    i = pl.program_id(0)
    j = pl.program_id(1)

    def normalize(x_ref):
        x = x_ref[...]
        ms = jnp.mean(x * x, axis=-1, keepdims=True)
        hn_ref[...] = (x * lax.rsqrt(ms + EPS) * g_ref[...]).astype(BF16)

    @pl.when(jnp.logical_and(j == 0, i < n_prompt_blocks))
    def _():
        normalize(xp_ref)

    @pl.when(jnp.logical_and(j == 0, i >= n_prompt_blocks))
    def _():
        normalize(xs_ref)

    def rope_heads(gain):
        def epilogue(acc, col0):
            reps = IN_NC // HEAD_DIM
            cos = jnp.concatenate([cos_ref[...]] * reps, axis=1)
            sin = jnp.concatenate([sin_ref[...]] * reps, axis=1)
            ms = jnp.dot((acc * acc).astype(BF16), hsum_ref[...], preferred_element_type=F32)
            an = acc * lax.rsqrt(ms + EPS) * gain
            rot = jnp.dot(an.astype(BF16), rot_ref[...], preferred_element_type=F32)
            o_ref[:, col0:col0 + IN_NC] = (an * cos + rot * sin).astype(BF16)
        return epilogue

    def plain(fn):
        def epilogue(acc, col0):
            o_ref[:, col0:col0 + IN_NC] = fn(acc).astype(BF16)
        return epilogue

    def tiles(epilogues):
        def chunk_dot(c):
            return jnp.dot(hn_ref[...], w_ref[:, c * IN_NC:(c + 1) * IN_NC], preferred_element_type=F32)

        acc = chunk_dot(0)
        for c, epilogue in enumerate(epilogues):
            nxt = chunk_dot(c + 1) if c + 1 < len(epilogues) else None
            epilogue(acc, c * IN_NC)
            acc = nxt

    n_chunks = IN_TN // IN_NC
    jkv = COL_K // IN_TN
    ju = COL_U // IN_TN
    jg = COL_GA // IN_TN

    @pl.when(j < jkv)
    def _():
        tiles([rope_heads(qg_ref[...])] * n_chunks)

    @pl.when(j == jkv)
    def _():
        nk = KV_COLS // IN_NC
        tiles([rope_heads(kg_ref[...])] * nk + [plain(lambda a: a)] * (n_chunks - nk))

    @pl.when(jnp.logical_and(j >= ju, j < jg))
    def _():
        tiles([plain(jax.nn.gelu)] * n_chunks)

    @pl.when(j >= jg)
    def _():
        tiles([plain(jax.nn.sigmoid)] * n_chunks)


def _in_proj(pos_blk, x_p, x_s, g_mix, w_in_r, q_gain, k_gain, cos_t, sin_t, head_sum, head_rot):
    npb = x_p.shape[0] // IN_TM
    t = x_p.shape[0] + x_s.shape[0]
    const = lambda i, j, pb: (0, 0)
    grid_spec = pltpu.PrefetchScalarGridSpec(
        num_scalar_prefetch=1,
        grid=(t // IN_TM, IN_COLS // IN_TN),
        in_specs=[
            pl.BlockSpec((IN_TM, D_MODEL), lambda i, j, pb: (jnp.minimum(i, npb - 1), 0)),
            pl.BlockSpec((IN_TM, D_MODEL), lambda i, j, pb: (jnp.maximum(i - npb, 0), 0)),
            pl.BlockSpec((1, D_MODEL), const),
            pl.BlockSpec((D_MODEL, IN_TN), lambda i, j, pb: (0, j)),
            pl.BlockSpec((1, IN_NC), const),
            pl.BlockSpec((1, IN_NC), const),
            pl.BlockSpec((IN_TM, HEAD_DIM), lambda i, j, pb: (pb[i], 0)),
            pl.BlockSpec((IN_TM, HEAD_DIM), lambda i, j, pb: (pb[i], 0)),
            pl.BlockSpec((IN_NC, IN_NC), const),
            pl.BlockSpec((IN_NC, IN_NC), const),
        ],
        out_specs=pl.BlockSpec((IN_TM, IN_TN), lambda i, j, pb: (i, j)),
        scratch_shapes=[pltpu.VMEM((IN_TM, D_MODEL), BF16)],
    )
    return pl.pallas_call(
        functools.partial(_in_proj_kernel, n_prompt_blocks=npb),
        grid_spec=grid_spec,
        out_shape=jax.ShapeDtypeStruct((t, IN_COLS), BF16),
        compiler_params=_params(("parallel", "arbitrary")),
        name="in_proj",
    )(pos_blk, x_p, x_s, g_mix, w_in_r, q_gain, k_gain, cos_t, sin_t, head_sum, head_rot)


def _attn_kernel(first_ref, last_ref, sinks_ref, q_ref, kp_ref, km_ref, kn_ref,
                 vp_ref, vm_ref, vn_ref, o_ref):
    i = pl.program_id(0)
    kv = pl.program_id(1)
    nqb = ATT_QB // WINDOW
    rows = GROUP * WINDOW
    kband = jnp.concatenate([kp_ref[...], km_ref[...], kn_ref[...]], axis=0)
    vband = jnp.concatenate([vp_ref[...], vm_ref[...], vn_ref[...]], axis=0)
    r = lax.broadcasted_iota(jnp.int32, (rows, 3 * WINDOW), 0) & (WINDOW - 1)
    c = lax.broadcasted_iota(jnp.int32, (rows, 3 * WINDOW), 1)
    d = c - r
    band_ok = jnp.logical_and(d >= 0, d <= 2 * WINDOW)
    lo = jnp.where(first_ref[i] != 0, WINDOW, 0)
    hi = jnp.where(last_ref[i] != 0, 2 * WINDOW, 3 * WINDOW)
    head = lax.broadcasted_iota(jnp.int32, (rows, 1), 0) >> (WINDOW.bit_length() - 1)
    sink = jnp.zeros((rows, 1), F32)
    for g in range(GROUP):
        sink = jnp.where(head == g, sinks_ref[kv * GROUP + g], sink)
    def scores(b):
        kb = kband[b * WINDOW:(b + 3) * WINDOW]
        qrows = slice(b * WINDOW, (b + 1) * WINDOW)
        qs = jnp.concatenate([q_ref[qrows, g * HEAD_DIM:(g + 1) * HEAD_DIM] for g in range(GROUP)], axis=0)
        return lax.dot_general(qs, kb, (((1,), (1,)), ((), ())), preferred_element_type=F32)

    s_next = scores(0)
    for b in range(nqb):
        ok = band_ok
        if b == 0:
            ok = jnp.logical_and(ok, c >= lo)
        if b == nqb - 1:
            ok = jnp.logical_and(ok, c < hi)
        vb = vband[b * WINDOW:(b + 3) * WINDOW]
        qrows = slice(b * WINDOW, (b + 1) * WINDOW)
        s = s_next
        if b + 1 < nqb:
            s_next = scores(b + 1)
        s = jnp.where(ok, s, NEG_INF)
        m = jnp.maximum(jnp.max(s, axis=-1, keepdims=True), sink)
        p = jnp.exp2(s - m)
        den = jnp.sum(p, axis=-1, keepdims=True) + jnp.exp2(sink - m)
        o = jnp.dot(p.astype(BF16), vb, preferred_element_type=F32) / den
        for g in range(GROUP):
            o_ref[qrows, g * HEAD_DIM:(g + 1) * HEAD_DIM] = o[g * WINDOW:(g + 1) * WINDOW].astype(BF16)


def _attention(proj, sinks, first, last):
    t = proj.shape[0]
    nqb = ATT_QB // WINDOW
    nrb = t // WINDOW
    kcol = COL_K // HEAD_DIM
    vcol = COL_V // HEAD_DIM
    qspec = pl.BlockSpec((ATT_QB, GROUP * HEAD_DIM), lambda i, kv, *_: (i, kv))

    def band_specs(col):
        return [
            pl.BlockSpec((WINDOW, HEAD_DIM), lambda i, kv, *_: (jnp.maximum(i * nqb - 1, 0), col + kv)),
            pl.BlockSpec((ATT_QB, HEAD_DIM), lambda i, kv, *_: (i, col + kv)),
            pl.BlockSpec((WINDOW, HEAD_DIM), lambda i, kv, *_: (jnp.minimum((i + 1) * nqb, nrb - 1), col + kv)),
        ]

    grid_spec = pltpu.PrefetchScalarGridSpec(
        num_scalar_prefetch=3,
        grid=(t // ATT_QB, N_KV_HEADS),
        in_specs=[qspec] + band_specs(kcol) + band_specs(vcol),
        out_specs=pl.BlockSpec((ATT_QB, GROUP * HEAD_DIM), lambda i, kv, *_: (i, kv)),
    )
    return pl.pallas_call(
        _attn_kernel,
        grid_spec=grid_spec,
        out_shape=jax.ShapeDtypeStruct((t, Q_COLS), BF16),
        compiler_params=_params(("parallel", "arbitrary")),
        name="window_attn",
    )(first, last, sinks, proj, proj, proj, proj, proj, proj, proj)


def _mix_kernel(attn_ref, u0_ref, u1_ref, sv0_ref, sv1_ref, ga_ref, gs_ref, wa_ref, wb_ref, ws_ref, bs_ref,
                gsv_ref, o_ref, vn_ref, sg_ref):
    j = pl.program_id(1)
    half = SG_WIDTH // 2
    groups_per_half = half // SG_CHUNK

    def spatial_gating():
        sv0 = sv0_ref[...].astype(F32)
        sv1 = sv1_ref[...].astype(F32)
        ms = (jnp.sum(sv0 * sv0, axis=-1, keepdims=True) + jnp.sum(sv1 * sv1, axis=-1, keepdims=True)) / SG_WIDTH
        r = lax.rsqrt(ms + EPS)
        vn_ref[:, :half] = (sv0 * r * gsv_ref[:, :half]).astype(BF16)
        vn_ref[:, half:] = (sv1 * r * gsv_ref[:, half:]).astype(BF16)
        for c in range(MIX_TM // SG_CHUNK):
            rows = slice(c * SG_CHUNK, (c + 1) * SG_CHUNK)
            for g in range(SG_GROUPS):
                cols = slice(g * SG_CHUNK, (g + 1) * SG_CHUNK)
                u_ref = u0_ref if g < groups_per_half else u1_ref
                g_in = g % groups_per_half
                u = u_ref[rows, g_in * SG_CHUNK:(g_in + 1) * SG_CHUNK]
                mixed = jnp.dot(ws_ref[g], vn_ref[rows, cols], preferred_element_type=F32) + bs_ref[g]
                sg_ref[rows, cols] = (u.astype(F32) * mixed).astype(BF16)

    def project(first):
        pa = jnp.dot(attn_ref[...], wa_ref[...], preferred_element_type=F32)
        if first:
            spatial_gating()
        ps = jnp.dot(sg_ref[...], wb_ref[...], preferred_element_type=F32)
        merged = ga_ref[...].astype(F32) * pa + gs_ref[...].astype(F32) * ps
        o_ref[...] = merged.astype(BF16)

    @pl.when(j == 0)
    def _():
        project(True)

    @pl.when(j > 0)
    def _():
        project(False)


def _mix(attn, proj, wa, wb, ws, bs_b, gsv):
    t = attn.shape[0]
    grid = (t // MIX_TM, D_MODEL // MIX_TN)
    ga0 = COL_GA // MIX_TN
    gs0 = COL_GS // MIX_TN
    half = SG_WIDTH // 2
    u0 = COL_U // half
    sv0 = COL_SV // half
    return pl.pallas_call(
        _mix_kernel,
        grid=grid,
        in_specs=[
            pl.BlockSpec((MIX_TM, Q_COLS), lambda i, j: (i, 0)),
            pl.BlockSpec((MIX_TM, half), lambda i, j: (i, u0)),
            pl.BlockSpec((MIX_TM, half), lambda i, j: (i, u0 + 1)),
            pl.BlockSpec((MIX_TM, half), lambda i, j: (i, sv0)),
            pl.BlockSpec((MIX_TM, half), lambda i, j: (i, sv0 + 1)),
            pl.BlockSpec((MIX_TM, MIX_TN), lambda i, j: (i, ga0 + j)),
            pl.BlockSpec((MIX_TM, MIX_TN), lambda i, j: (i, gs0 + j)),
            pl.BlockSpec((Q_COLS, MIX_TN), lambda i, j: (0, j)),
            pl.BlockSpec((SG_WIDTH, MIX_TN), lambda i, j: (0, j)),
            pl.BlockSpec((SG_GROUPS, SG_CHUNK, SG_CHUNK), lambda i, j: (0, 0, 0)),
            pl.BlockSpec((SG_GROUPS, SG_CHUNK, SG_CHUNK), lambda i, j: (0, 0, 0)),
            pl.BlockSpec((1, SG_WIDTH), lambda i, j: (0, 0)),
        ],
        out_specs=pl.BlockSpec((MIX_TM, MIX_TN), lambda i, j: (i, j)),
        out_shape=jax.ShapeDtypeStruct((t, D_MODEL), BF16),
        scratch_shapes=[pltpu.VMEM((MIX_TM, SG_WIDTH), BF16), pltpu.VMEM((MIX_TM, SG_WIDTH), BF16)],
        compiler_params=_params(("parallel", "arbitrary")),
        name="branch_mix",
    )(attn, proj, proj, proj, proj, proj, proj, wa, wb, ws, bs_b, gsv)


def _out_proj_kernel(m_ref, w_ref, xp_ref, xs_ref, o_ref, *, n_prompt_blocks):
    i = pl.program_id(0)
    y = jnp.dot(m_ref[...], w_ref[...], preferred_element_type=F32)

    @pl.when(i < n_prompt_blocks)
    def _():
        o_ref[...] = xp_ref[...] + y

    @pl.when(i >= n_prompt_blocks)
    def _():
        o_ref[...] = xs_ref[...] + y


def _out_proj(merged, w_o, x_p, x_s):
    t = merged.shape[0]
    npb = x_p.shape[0] // OUT_TM
    nj = D_MODEL // OUT_TN
    return pl.pallas_call(
        functools.partial(_out_proj_kernel, n_prompt_blocks=npb),
        grid=(t // OUT_TM, nj),
        in_specs=[
            pl.BlockSpec((OUT_TM, D_MODEL), lambda i, j: (i, 0)),
            pl.BlockSpec((D_MODEL, OUT_TN), lambda i, j: (0, j)),
            pl.BlockSpec((OUT_TM, OUT_TN), lambda i, j: (jnp.minimum(i, npb - 1), jnp.where(i < npb, j, nj - 1))),
            pl.BlockSpec((OUT_TM, OUT_TN), lambda i, j: (jnp.maximum(i - npb, 0), jnp.where(i < npb, 0, j))),
        ],
        out_specs=pl.BlockSpec((OUT_TM, OUT_TN), lambda i, j: (i, j)),
        out_shape=jax.ShapeDtypeStruct((t, D_MODEL), F32),
        compiler_params=_params(("parallel", "arbitrary")),
        name="out_proj",
    )(merged, w_o, x_p, x_s)


def _split_bf16(a):
    hi = a.astype(BF16)
    lo = (a - hi.astype(F32)).astype(BF16)
    return hi, lo


def _router_kernel(x_ref, g_ref, wr_ref, br_ref, idx_ref, wgt_ref, rank_ref, cnt_ref, carry_ref):
    step = pl.program_id(0)

    @pl.when(step == 0)
    def _():
        carry_ref[...] = jnp.zeros_like(carry_ref)

    x = x_ref[...]
    ms = jnp.mean(x * x, axis=-1, keepdims=True)
    h = x * lax.rsqrt(ms + EPS) * g_ref[...]

    h_hi, h_lo = _split_bf16(h)
    w_hi, w_lo = _split_bf16(wr_ref[...])
    dn = (((1,), (1,)), ((), ()))
    logits = (lax.dot_general(w_hi, h_hi, dn, preferred_element_type=F32)
              + lax.dot_general(w_hi, h_lo, dn, preferred_element_type=F32)
              + lax.dot_general(w_lo, h_hi, dn, preferred_element_type=F32))
    logits = logits + br_ref[...]

    e_iota = lax.broadcasted_iota(jnp.int32, (N_EXPERTS, RT_TM), 0)
    vals = logits
    tops, idxs = [], []
    onehot = jnp.zeros((N_EXPERTS, RT_TM), F32)
    for _ in range(TOP_K):
        m = jnp.max(vals, axis=0, keepdims=True)
        idx = jnp.min(jnp.where(vals == m, e_iota, N_EXPERTS), axis=0, keepdims=True)
        sel = e_iota == idx
        onehot = onehot + sel.astype(F32)
        vals = jnp.where(sel, -jnp.inf, vals)
        tops.append(m)
        idxs.append(idx)

    exps = [jnp.exp(v - tops[0]) for v in tops]
    den = exps[0] + exps[1] + exps[2] + exps[3]

    tr = lax.broadcasted_iota(jnp.int32, (RT_TM, RT_TM), 0)
    tc = lax.broadcasted_iota(jnp.int32, (RT_TM, RT_TM), 1)
    upper = jnp.where(tr < tc, 1.0, 0.0).astype(BF16)
    before = jnp.dot(onehot.astype(BF16), upper, preferred_element_type=F32) + carry_ref[:, 0:1]

    for k in range(TOP_K):
        sel = e_iota == idxs[k]
        rank = jnp.sum(jnp.where(sel, before, 0.0), axis=0, keepdims=True)
        idx_ref[k:k + 1, :] = idxs[k]
        wgt_ref[k:k + 1, :] = exps[k] / den
        rank_ref[k:k + 1, :] = rank.astype(jnp.int32)

    carry_ref[...] = carry_ref[...] + jnp.sum(onehot, axis=1, keepdims=True)
    cnt_ref[...] = carry_ref[...]


def _router(x1, g_ffn, w_router_t, b_router):
    t = x1.shape[0]
    return pl.pallas_call(
        _router_kernel,
        grid=(t // RT_TM,),
        in_specs=[
            pl.BlockSpec((RT_TM, D_MODEL), lambda i: (i, 0)),
            pl.BlockSpec((1, D_MODEL), lambda i: (0, 0)),
            pl.BlockSpec((N_EXPERTS, D_MODEL), lambda i: (0, 0)),
            pl.BlockSpec((N_EXPERTS, 1), lambda i: (0, 0)),
        ],
        out_specs=[
            pl.BlockSpec((TOP_K, RT_TM), lambda i: (0, i)),
            pl.BlockSpec((TOP_K, RT_TM), lambda i: (0, i)),
            pl.BlockSpec((TOP_K, RT_TM), lambda i: (0, i)),
            pl.BlockSpec((N_EXPERTS, 128), lambda i: (0, 0)),
        ],
        out_shape=[
            jax.ShapeDtypeStruct((TOP_K, t), jnp.int32),
            jax.ShapeDtypeStruct((TOP_K, t), F32),
            jax.ShapeDtypeStruct((TOP_K, t), jnp.int32),
            jax.ShapeDtypeStruct((N_EXPERTS, 128), F32),
        ],
        scratch_shapes=[pltpu.VMEM((N_EXPERTS, 128), F32)],
        compiler_params=_params(("arbitrary",)),
        name="router",
    )(x1, g_ffn, w_router_t, b_router)


def _row_block_wait(src, dst, sem, copies):
    for _ in range(copies):
        pltpu.make_async_copy(src, dst, sem).wait()


def _dispatch_kernel(padstart_ref, padcnt_ref, dest_ref, x_ref, g_ref, xs_ref, hbuf, zrow, sems, zsem):
    i = pl.program_id(0)
    n = pl.num_programs(0)
    slot = i % 2

    def zero_copy(row):
        return pltpu.make_async_copy(zrow.at[pl.ds(0, 1), :], xs_ref.at[pl.ds(row, 1), :], zsem)

    @pl.when(i == 0)
    def _():
        zrow[...] = jnp.zeros_like(zrow)
        for e in range(N_EXPERTS):
            def start(r, c, e=e):
                zero_copy(padstart_ref[e] + r).start()
                return c
            lax.fori_loop(0, padcnt_ref[e], start, 0)

    x = x_ref[...]
    ms = jnp.mean(x * x, axis=-1, keepdims=True)
    h = x * lax.rsqrt(ms + EPS) * g_ref[...]
    hbuf[slot] = _pack_pair(h[:, :HALF], h[:, HALF:])

    def send(r, c):
        for k in range(TOP_K):
            d = dest_ref[0, 0, k * DSP_TB + r]
            pltpu.make_async_copy(hbuf.at[slot, pl.ds(r, 1), :], xs_ref.at[pl.ds(d, 1), :],
                                  sems.at[slot]).start(priority=k % 2)
        return c

    lax.fori_loop(0, DSP_TB, send, 0)

    def wait_slot(s):
        _row_block_wait(hbuf.at[s], xs_ref.at[pl.ds(0, DSP_TB), :], sems.at[s], TOP_K)

    @pl.when(i > 0)
    def _():
        wait_slot(1 - slot)

    @pl.when(i == n - 1)
    def _():
        wait_slot(slot)
        for e in range(N_EXPERTS):
            def done(r, c):
                zero_copy(0).wait()
                return c
            lax.fori_loop(0, padcnt_ref[e], done, 0)


def _dispatch(padstart, padcnt, dest_blocks, x1, g_ffn, n_rows):
    t = x1.shape[0]
    grid_spec = pltpu.PrefetchScalarGridSpec(
        num_scalar_prefetch=2,
        grid=(t // DSP_TB,),
        in_specs=[
            pl.BlockSpec((1, 1, TOP_K * DSP_TB), lambda i, *_: (i, 0, 0), memory_space=pltpu.SMEM),
            pl.BlockSpec((DSP_TB, D_MODEL), lambda i, *_: (i, 0)),
            pl.BlockSpec((1, D_MODEL), lambda i, *_: (0, 0)),
        ],
        out_specs=pl.BlockSpec(memory_space=pl.ANY),
        scratch_shapes=[
            pltpu.VMEM((2, DSP_TB, HALF), U32),
            pltpu.VMEM((8, HALF), U32),
            pltpu.SemaphoreType.DMA((2,)),
            pltpu.SemaphoreType.DMA(()),
        ],
    )
    return pl.pallas_call(
        _dispatch_kernel,
        grid_spec=grid_spec,
        out_shape=jax.ShapeDtypeStruct((n_rows, HALF), U32),
        compiler_params=_params(("arbitrary",)),
        name="moe_dispatch",
    )(padstart, padcnt, dest_blocks, x1, g_ffn)


def _moe_kernel(be_ref, ns_ref, na_ref, x_ref, wg_ref, wu_ref, bg_ref, bu_ref, wd_ref, bd_ref, o_ref,
                acc_ref, xb_ref):
    b = pl.program_id(0)
    j = pl.program_id(1)
    nj = pl.num_programs(1)
    nsub = ns_ref[b]

    n_full = MOE_BM // MOE_SUB
    full = nsub == n_full
    whole = slice(None)

    def rows_of(s):
        return pl.ds(pl.multiple_of(s * MOE_SUB, MOE_SUB), MOE_SUB)

    def for_rows(fn):
        @pl.when(full)
        def _():
            fn(whole, MOE_BM)

        @pl.when(jnp.logical_not(full))
        def _():
            def body(s, c):
                fn(rows_of(s), MOE_SUB)
                return c
            lax.fori_loop(0, nsub, body, 0)

    def unpack(rows, m):
        lo, hi = _unpack_pair(x_ref[rows, :])
        xb_ref[0, rows, :] = lo.astype(BF16)
        xb_ref[1, rows, :] = hi.astype(BF16)
        acc_ref[rows, :] = jnp.zeros((m, D_MODEL), F32)

    def expert_mlp(rows, m):
        wg = wg_ref[...].astype(BF16)
        wu = wu_ref[...].astype(BF16)
        x0 = xb_ref[0, rows, :]
        x1 = xb_ref[1, rows, :]
        gate = (jnp.dot(x0, wg[:HALF], preferred_element_type=F32)
                + jnp.dot(x1, wg[HALF:], preferred_element_type=F32) + bg_ref[...])
        up = (jnp.dot(x0, wu[:HALF], preferred_element_type=F32)
              + jnp.dot(x1, wu[HALF:], preferred_element_type=F32) + bu_ref[...])
        gate = jnp.minimum(gate, SWIGLU_LIMIT)
        up = jnp.clip(up, -SWIGLU_LIMIT, SWIGLU_LIMIT)
        act = (up + 1.0) * gate * jax.nn.sigmoid(SWIGLU_ALPHA * gate)
        acc_ref[rows, :] += jnp.dot(act.astype(BF16), wd_ref[...].astype(BF16), preferred_element_type=F32)

    def finish(rows, m):
        y = acc_ref[rows, :] + bd_ref[...]
        o_ref[rows, :] = _pack_pair(y[:, :HALF], y[:, HALF:])

    @pl.when(j == 0)
    def _():
        for_rows(unpack)

    for_rows(expert_mlp)

    @pl.when(j == nj - 1)
    def _():
        for_rows(finish)

        def blank(s, c):
            o_ref[rows_of(s), :] = jnp.zeros((MOE_SUB, HALF), U32)
            return c
        lax.fori_loop(nsub, n_full, blank, 0)


def _moe(block_e, block_nsub, n_active, xs, w_gu, b_gu, w_down, b_down):
    n_rows = xs.shape[0]
    nfc = D_FF // MOE_FC

    def frozen_j(b, j, na):
        return jnp.where(b < na[0], j, nfc - 1)

    grid_spec = pltpu.PrefetchScalarGridSpec(
        num_scalar_prefetch=3,
        grid=(n_rows // MOE_BM, nfc),
        in_specs=[
            pl.BlockSpec((MOE_BM, HALF), lambda b, j, be, ns, na: (jnp.minimum(b, na[0] - 1), 0)),
            pl.BlockSpec((None, D_MODEL, MOE_FC), lambda b, j, be, ns, na: (be[b], 0, frozen_j(b, j, na))),
            pl.BlockSpec((None, D_MODEL, MOE_FC), lambda b, j, be, ns, na: (be[b], 0, nfc + frozen_j(b, j, na))),
            pl.BlockSpec((None, 1, MOE_FC), lambda b, j, be, ns, na: (be[b], 0, frozen_j(b, j, na))),
            pl.BlockSpec((None, 1, MOE_FC), lambda b, j, be, ns, na: (be[b], 0, nfc + frozen_j(b, j, na))),
            pl.BlockSpec((None, MOE_FC, D_MODEL), lambda b, j, be, ns, na: (be[b], frozen_j(b, j, na), 0)),
            pl.BlockSpec((None, 1, D_MODEL), lambda b, j, be, ns, na: (be[b], 0, 0)),
        ],
        out_specs=pl.BlockSpec((MOE_BM, HALF), lambda b, j, be, ns, na: (b, 0)),
        scratch_shapes=[pltpu.VMEM((MOE_BM, D_MODEL), F32), pltpu.VMEM((2, MOE_BM, HALF), BF16)],
    )
    return pl.pallas_call(
        _moe_kernel,
        grid_spec=grid_spec,
        out_shape=jax.ShapeDtypeStruct((n_rows, HALF), U32),
        compiler_params=_params(("arbitrary", "arbitrary")),
        name="moe_experts",
    )(block_e, block_nsub, n_active, xs, w_gu, w_gu, b_gu, b_gu, w_down, b_down)


def _combine_ple_kernel(dcur_ref, dnext_ref, x_ref, wt_ref, ys_ref, g_ref, pp_ref, ps_ref, wg_ref, wp_ref,
                        op_ref, os_ref, buf, sems, *, n_prompt_blocks):
    i = pl.program_id(0)
    n = pl.num_programs(0)
    slot = i % 2

    def fetch_row(dref, s, r):
        for k in range(TOP_K):
            d = dref[0, 0, k * CMB_TB + r]
            pltpu.make_async_copy(ys_ref.at[pl.ds(d, 1), :], buf.at[s, k, pl.ds(r, 1), :],
                                  sems.at[s]).start(priority=k % 2)

    def wait_slot(s):
        _row_block_wait(ys_ref.at[pl.ds(0, CMB_TB), :], buf.at[s, 0], sems.at[s], TOP_K)

    @pl.when(i == 0)
    def _():
        def body(r, c):
            fetch_row(dcur_ref, 0, r)
            return c
        lax.fori_loop(0, CMB_TB, body, 0)

    wait_slot(slot)

    groups = iter(np.array_split(np.arange(CMB_TB), PLE_CHUNKS))

    def issue_group():
        for row in next(groups):
            fetch_row(dnext_ref, 1 - slot, int(row))

    x_lo = x_ref[:, :HALF]
    x_hi = x_ref[:, HALF:]
    for k in range(TOP_K):
        lo, hi = _unpack_pair(buf[slot, k])
        w = wt_ref[:, k:k + 1]
        x_lo = x_lo + w * lo
        x_hi = x_hi + w * hi

    ms = (jnp.sum(x_lo * x_lo, axis=-1, keepdims=True) + jnp.sum(x_hi * x_hi, axis=-1, keepdims=True)) / D_MODEL
    r = lax.rsqrt(ms + EPS)
    h_lo = (x_lo * r * g_ref[:, :HALF]).astype(BF16)
    h_hi = (x_hi * r * g_ref[:, HALF:]).astype(BF16)

    cols_per_chunk = D_MODEL // PLE_CHUNKS
    gates = []
    for c in range(PLE_CHUNKS):
        issue_group()
        cols = slice(c * cols_per_chunk, (c + 1) * cols_per_chunk)
        gates.append(jax.nn.sigmoid(jnp.dot(h_lo, wg_ref[:HALF, cols], preferred_element_type=F32)
                                    + jnp.dot(h_hi, wg_ref[HALF:, cols], preferred_element_type=F32)))
    gate = jnp.concatenate(gates, axis=1)

    def finish(p_ref, o_ref):
        pp = jnp.dot(p_ref[...].astype(BF16), wp_ref[...], preferred_element_type=F32)
        o_ref[:, :HALF] = x_lo + gate[:, :HALF] * pp[:, :HALF]
        o_ref[:, HALF:] = x_hi + gate[:, HALF:] * pp[:, HALF:]

    @pl.when(i < n_prompt_blocks)
    def _():
        finish(pp_ref, op_ref)

    @pl.when(i >= n_prompt_blocks)
    def _():
        finish(ps_ref, os_ref)

    @pl.when(i == n - 1)
    def _():
        wait_slot(1 - slot)


def _combine_ple(dest_blocks, x1, wgt_t, ys, g_ple, p_p, p_s, w_gate, w_proj):
    t = x1.shape[0]
    nb = t // CMB_TB
    npb = p_p.shape[0] // CMB_TB
    const = lambda i: (0, 0)
    prompt_rows = lambda i: (jnp.minimum(i, npb - 1), 0)
    sample_rows = lambda i: (jnp.maximum(i - npb, 0), 0)
    return pl.pallas_call(
        functools.partial(_combine_ple_kernel, n_prompt_blocks=npb),
        grid=(nb,),
        in_specs=[
            pl.BlockSpec((1, 1, TOP_K * CMB_TB), lambda i: (i, 0, 0), memory_space=pltpu.SMEM),
            pl.BlockSpec((1, 1, TOP_K * CMB_TB), lambda i: (jnp.minimum(i + 1, nb - 1), 0, 0),
                         memory_space=pltpu.SMEM),
            pl.BlockSpec((CMB_TB, D_MODEL), lambda i: (i, 0)),
            pl.BlockSpec((CMB_TB, TOP_K), lambda i: (i, 0)),
            pl.BlockSpec(memory_space=pl.ANY),
            pl.BlockSpec((1, D_MODEL), const),
            pl.BlockSpec((CMB_TB, PLE_DIM), prompt_rows),
            pl.BlockSpec((CMB_TB, PLE_DIM), sample_rows),
            pl.BlockSpec((D_MODEL, D_MODEL), const),
            pl.BlockSpec((PLE_DIM, D_MODEL), const),
        ],
        out_specs=[
            pl.BlockSpec((CMB_TB, D_MODEL), prompt_rows),
            pl.BlockSpec((CMB_TB, D_MODEL), sample_rows),
        ],
        out_shape=[
            jax.ShapeDtypeStruct((p_p.shape[0], D_MODEL), F32),
            jax.ShapeDtypeStruct((p_s.shape[0], D_MODEL), F32),
        ],
        scratch_shapes=[
            pltpu.VMEM((2, TOP_K, CMB_TB, HALF), U32),
            pltpu.SemaphoreType.DMA((2,)),
        ],
        compiler_params=_params(("arbitrary",)),
        name="moe_combine_ple",
    )(dest_blocks, dest_blocks, x1, wgt_t, ys, g_ple, p_p, p_s, w_gate, w_proj)


def _rope_tables(max_len):
    half = HEAD_DIM // 2
    inv_freq = ROPE_THETA ** (-jnp.arange(half, dtype=F32) / half)
    ang = jnp.arange(max_len, dtype=F32)[:, None] * inv_freq[None, :]
    cos, sin = jnp.cos(ang), jnp.sin(ang)
    return jnp.concatenate([cos, cos], axis=-1), jnp.concatenate([-sin, sin], axis=-1)


def _head_matrices():
    col = np.arange(IN_NC)
    same_head = (col[:, None] // HEAD_DIM) == (col[None, :] // HEAD_DIM)
    head_sum = np.where(same_head, 1.0 / HEAD_DIM, 0.0)
    src = (col // HEAD_DIM) * HEAD_DIM + (col % HEAD_DIM + HEAD_DIM // 2) % HEAD_DIM
    head_rot = (col[:, None] == src[None, :]).astype(np.float32)
    return jnp.asarray(head_sum, BF16), jnp.asarray(head_rot, BF16)


def _seq_block_tables(seq_lens):
    pos, first, last = [], [], []
    for s in seq_lens:
        pos += list(range(s // IN_TM))
        nb = s // ATT_QB
        first += [1] + [0] * (nb - 1)
        last += [0] * (nb - 1) + [1]
    return jnp.array(pos, jnp.int32), jnp.array(first, jnp.int32), jnp.array(last, jnp.int32)


def _layer(x_p, x_s, p_p, p_s, seq_lens, g_mix, w_in, q_norm, k_norm, sinks, g_sg_v, w_s, b_s, w_branch, w_o,
           g_ffn, w_router, b_router, w_gu, b_gu, w_down, b_down, g_ple, w_ple_gate, w_ple_proj):
    t = x_p.shape[0] + x_s.shape[0]
    w_in_r = w_in.astype(BF16)
    cos_t, sin_t = _rope_tables(max(seq_lens))
    head_sum, head_rot = _head_matrices()
    pos_blk, first, last = _seq_block_tables(seq_lens)
    reps = IN_NC // HEAD_DIM
    q_gain = jnp.tile(q_norm * (HEAD_DIM ** -0.5 * LOG2E), reps)[None, :]
    k_gain = jnp.tile(k_norm, reps)[None, :]

    proj = _in_proj(pos_blk, x_p, x_s, g_mix[None, :], w_in_r, q_gain, k_gain, cos_t, sin_t, head_sum, head_rot)
    attn = _attention(proj, sinks * LOG2E, first, last)
    bs_b = jnp.broadcast_to(b_s[:, :, None], (SG_GROUPS, SG_CHUNK, SG_CHUNK))
    merged = _mix(attn, proj, w_branch[0].astype(BF16), w_branch[1].astype(BF16), w_s.astype(BF16), bs_b,
                  g_sg_v[None, :])
    x1 = _out_proj(merged, w_o.astype(BF16), x_p, x_s)

    idx, wgt, rank, cnt = _router(x1, g_ffn[None, :], w_router.T, b_router[:, None])

    counts = cnt[:, 0].astype(jnp.int32)
    padded = (counts + MOE_BM - 1) // MOE_BM * MOE_BM
    padded_end = jnp.cumsum(padded)
    padded_start = padded_end - padded
    experts = jnp.arange(N_EXPERTS, dtype=jnp.int32)
    dest = rank + jnp.sum(jnp.where(idx[:, :, None] == experts, padded_start, 0), axis=-1)
    n_blocks = (t * TOP_K) // MOE_BM + N_EXPERTS
    block_row = jnp.arange(n_blocks, dtype=jnp.int32) * MOE_BM
    n_active = padded_end[-1] // MOE_BM
    block_e = jnp.minimum(jnp.sum(padded_end[None, :] <= block_row[:, None], axis=1), N_EXPERTS - 1)
    block_valid = jnp.clip(padded_start[block_e] + counts[block_e] - block_row, 0, MOE_BM)
    block_nsub = ((block_valid + MOE_SUB - 1) // MOE_SUB).astype(jnp.int32)
    block_e = jnp.where(block_row < padded_end[-1], block_e, block_e[n_active - 1]).astype(jnp.int32)

    def token_blocks(tb):
        return dest.reshape(TOP_K, t // tb, tb).transpose(1, 0, 2).reshape(t // tb, 1, TOP_K * tb)

    xs = _dispatch(padded_start + counts, (-counts) % MOE_SUB, token_blocks(DSP_TB), x1, g_ffn[None, :],
                   n_blocks * MOE_BM)
    ys = _moe(block_e, block_nsub, n_active[None].astype(jnp.int32), xs, w_gu, b_gu[:, None, :], w_down,
              b_down[:, None, :])
    return _combine_ple(token_blocks(CMB_TB), x1, wgt.T, ys, g_ple[None, :], p_p, p_s,
                        w_ple_gate.astype(BF16), w_ple_proj.astype(BF16))


def kernel(x_prompt, x_sample, p_prompt, p_sample, g_mix, w_in, q_norm, k_norm, sinks, g_sg_v, w_s, b_s,
           w_branch, w_o, g_ffn, w_router, b_router, w_gu, b_gu, w_down, b_down, g_ple, w_ple_gate,
           w_ple_proj):
    depth = g_mix.shape[0]
    bp, sp, d = x_prompt.shape
    bs, ss, _ = x_sample.shape
    seq_lens = [sp] * bp + [ss] * bs
    x_p = x_prompt.reshape(bp * sp, d)
    x_s = x_sample.reshape(bs * ss, d)
    for l in range(depth):
        x_p, x_s = _layer(x_p, x_s, p_prompt[l].reshape(bp * sp, PLE_DIM), p_sample[l].reshape(bs * ss, PLE_DIM),
                          seq_lens, g_mix[l], w_in[l], q_norm[l], k_norm[l], sinks[l], g_sg_v[l], w_s[l], b_s[l],
                          w_branch[l], w_o[l], g_ffn[l], w_router[l], b_router[l], w_gu[l], b_gu[l], w_down[l],
                          b_down[l], g_ple[l], w_ple_gate[l], w_ple_proj[l])
    return x_p.reshape(bp, sp, d), x_s.reshape(bs, ss, d)
```

```python
import functools
import math

import jax
import jax.numpy as jnp
import numpy as np
from jax import lax
from jax.experimental import pallas as pl
from jax.experimental.pallas import tpu as pltpu

F32 = jnp.float32
BF16 = jnp.bfloat16
U32 = jnp.uint32

D_MODEL = 2048
HALF = D_MODEL // 2
HEAD_DIM = 128
N_HEADS = 16
N_KV_HEADS = 4
GROUP = N_HEADS // N_KV_HEADS
WINDOW = 128
ROPE_THETA = 10000.0
SG_GROUPS = 16
SG_WIDTH = 2048
SG_CHUNK = 128
N_EXPERTS = 32
TOP_K = 4
D_FF = 2048
SWIGLU_LIMIT = 7.0
SWIGLU_ALPHA = 1.702
PLE_DIM = 256
EPS = 1e-6
NEG_INF = -1e30
LOG2E = math.log2(math.e)

Q_COLS = N_HEADS * HEAD_DIM
KV_COLS = N_KV_HEADS * HEAD_DIM
IN_COLS = Q_COLS + 2 * KV_COLS + 2 * SG_WIDTH + 2 * D_MODEL

COL_Q = 0
COL_K = Q_COLS
COL_V = COL_K + KV_COLS
COL_U = COL_V + KV_COLS
COL_SV = COL_U + SG_WIDTH
COL_GA = COL_SV + SG_WIDTH
COL_GS = COL_GA + D_MODEL

VMEM_LIMIT = 56 * 1024 * 1024
MOE_VMEM_LIMIT = 62 * 1024 * 1024

IN_TM, IN_TN, IN_NC = 1024, 1024, 256
ATT_QB = 1024
MIX_TM, MIX_TN = 512, 1024
OUT_TM, OUT_TN = 1024, 1024
RT_TM = 512
MOE_BM, MOE_FC, MOE_SUB = 1024, 512, 256
DSP_TB = 256
CMB_TB = 256
PLE_CHUNKS = 8


def _params(sem, vmem_limit=VMEM_LIMIT):
    return pltpu.CompilerParams(dimension_semantics=sem, vmem_limit_bytes=vmem_limit)


def _pack_pair(lo, hi):
    def rounded_bits(x):
        return lax.bitcast_convert_type(x.astype(BF16).astype(F32), U32)
    return (rounded_bits(lo) >> 16) | rounded_bits(hi)


def _unpack_pair(u):
    lo = lax.bitcast_convert_type(u << 16, F32)
    hi = lax.bitcast_convert_type(u & U32(0xFFFF0000), F32)
    return lo, hi


def _in_proj_kernel(pos_ref, xp_ref, xs_ref, g_ref, w_ref, qg_ref, kg_ref, cos_ref, sin_ref, hsum_ref, rot_ref,
                    o_ref, hn_ref, *, n_prompt_blocks):
    i = pl.program_id(0)
    j = pl.program_id(1)

    def normalize(x_ref):
        x = x_ref[...]
        ms = jnp.mean(x * x, axis=-1, keepdims=True)
        hn_ref[...] = (x * lax.rsqrt(ms + EPS) * g_ref[...]).astype(BF16)

    @pl.when(jnp.logical_and(j == 0, i < n_prompt_blocks))
    def _():
        normalize(xp_ref)

    @pl.when(jnp.logical_and(j == 0, i >= n_prompt_blocks))
    def _():
        normalize(xs_ref)

    def rope_heads(gain):
        def epilogue(acc, col0):
            reps = IN_NC // HEAD_DIM
            cos = jnp.concatenate([cos_ref[...]] * reps, axis=1)
            sin = jnp.concatenate([sin_ref[...]] * reps, axis=1)
            ms = jnp.dot((acc * acc).astype(BF16), hsum_ref[...], preferred_element_type=F32)
            an = acc * lax.rsqrt(ms + EPS) * gain
            rot = jnp.dot(an.astype(BF16), rot_ref[...], preferred_element_type=F32)
            o_ref[:, col0:col0 + IN_NC] = (an * cos + rot * sin).astype(BF16)
        return epilogue

    def plain(fn):
        def epilogue(acc, col0):
            o_ref[:, col0:col0 + IN_NC] = fn(acc).astype(BF16)
        return epilogue

    def tiles(epilogues):
        def chunk_dot(c):
            return jnp.dot(hn_ref[...], w_ref[:, c * IN_NC:(c + 1) * IN_NC], preferred_element_type=F32)

        acc = chunk_dot(0)
        for c, epilogue in enumerate(epilogues):
            nxt = chunk_dot(c + 1) if c + 1 < len(epilogues) else None
            epilogue(acc, c * IN_NC)
            acc = nxt

    n_chunks = IN_TN // IN_NC
    jkv = COL_K // IN_TN
    ju = COL_U // IN_TN
    jg = COL_GA // IN_TN

    @pl.when(j < jkv)
    def _():
        tiles([rope_heads(qg_ref[...])] * n_chunks)

    @pl.when(j == jkv)
    def _():
        nk = KV_COLS // IN_NC
        tiles([rope_heads(kg_ref[...])] * nk + [plain(lambda a: a)] * (n_chunks - nk))

    @pl.when(jnp.logical_and(j >= ju, j < jg))
    def _():
        tiles([plain(jax.nn.gelu)] * n_chunks)

    @pl.when(j >= jg)
    def _():
        tiles([plain(jax.nn.sigmoid)] * n_chunks)


def _in_proj(pos_blk, x_p, x_s, g_mix, w_in_r, q_gain, k_gain, cos_t, sin_t, head_sum, head_rot):
    npb = x_p.shape[0] // IN_TM
    t = x_p.shape[0] + x_s.shape[0]
    const = lambda i, j, pb: (0, 0)
    grid_spec = pltpu.PrefetchScalarGridSpec(
        num_scalar_prefetch=1,
        grid=(t // IN_TM, IN_COLS // IN_TN),
        in_specs=[
            pl.BlockSpec((IN_TM, D_MODEL), lambda i, j, pb: (jnp.minimum(i, npb - 1), 0)),
            pl.BlockSpec((IN_TM, D_MODEL), lambda i, j, pb: (jnp.maximum(i - npb, 0), 0)),
            pl.BlockSpec((1, D_MODEL), const),
            pl.BlockSpec((D_MODEL, IN_TN), lambda i, j, pb: (0, j)),
            pl.BlockSpec((1, IN_NC), const),
            pl.BlockSpec((1, IN_NC), const),
            pl.BlockSpec((IN_TM, HEAD_DIM), lambda i, j, pb: (pb[i], 0)),
            pl.BlockSpec((IN_TM, HEAD_DIM), lambda i, j, pb: (pb[i], 0)),
            pl.BlockSpec((IN_NC, IN_NC), const),
            pl.BlockSpec((IN_NC, IN_NC), const),
        ],
        out_specs=pl.BlockSpec((IN_TM, IN_TN), lambda i, j, pb: (i, j)),
        scratch_shapes=[pltpu.VMEM((IN_TM, D_MODEL), BF16)],
    )
    return pl.pallas_call(
        functools.partial(_in_proj_kernel, n_prompt_blocks=npb),
        grid_spec=grid_spec,
        out_shape=jax.ShapeDtypeStruct((t, IN_COLS), BF16),
        compiler_params=_params(("parallel", "arbitrary")),
        name="in_proj",
    )(pos_blk, x_p, x_s, g_mix, w_in_r, q_gain, k_gain, cos_t, sin_t, head_sum, head_rot)


def _attn_kernel(first_ref, last_ref, sinks_ref, q_ref, kp_ref, km_ref, kn_ref,
                 vp_ref, vm_ref, vn_ref, o_ref):
    i = pl.program_id(0)
    kv = pl.program_id(1)
    nqb = ATT_QB // WINDOW
    rows = GROUP * WINDOW
    kband = jnp.concatenate([kp_ref[...], km_ref[...], kn_ref[...]], axis=0)
    vband = jnp.concatenate([vp_ref[...], vm_ref[...], vn_ref[...]], axis=0)
    r = lax.broadcasted_iota(jnp.int32, (rows, 3 * WINDOW), 0) & (WINDOW - 1)
    c = lax.broadcasted_iota(jnp.int32, (rows, 3 * WINDOW), 1)
    d = c - r
    band_ok = jnp.logical_and(d >= 0, d <= 2 * WINDOW)
    lo = jnp.where(first_ref[i] != 0, WINDOW, 0)
    hi = jnp.where(last_ref[i] != 0, 2 * WINDOW, 3 * WINDOW)
    head = lax.broadcasted_iota(jnp.int32, (rows, 1), 0) >> (WINDOW.bit_length() - 1)
    sink = jnp.zeros((rows, 1), F32)
    for g in range(GROUP):
        sink = jnp.where(head == g, sinks_ref[kv * GROUP + g], sink)
    def scores(b):
        kb = kband[b * WINDOW:(b + 3) * WINDOW]
        qrows = slice(b * WINDOW, (b + 1) * WINDOW)
        qs = jnp.concatenate([q_ref[qrows, g * HEAD_DIM:(g + 1) * HEAD_DIM] for g in range(GROUP)], axis=0)
        return lax.dot_general(qs, kb, (((1,), (1,)), ((), ())), preferred_element_type=F32)

    s_next = scores(0)
    for b in range(nqb):
        ok = band_ok
        if b == 0:
            ok = jnp.logical_and(ok, c >= lo)
        if b == nqb - 1:
            ok = jnp.logical_and(ok, c < hi)
        vb = vband[b * WINDOW:(b + 3) * WINDOW]
        qrows = slice(b * WINDOW, (b + 1) * WINDOW)
        s = s_next
        if b + 1 < nqb:
            s_next = scores(b + 1)
        s = jnp.where(ok, s, NEG_INF)
        m = jnp.maximum(jnp.max(s, axis=-1, keepdims=True), sink)
        p = jnp.exp2(s - m)
        den = jnp.sum(p, axis=-1, keepdims=True) + jnp.exp2(sink - m)
        o = jnp.dot(p.astype(BF16), vb, preferred_element_type=F32) / den
        for g in range(GROUP):
            o_ref[qrows, g * HEAD_DIM:(g + 1) * HEAD_DIM] = o[g * WINDOW:(g + 1) * WINDOW].astype(BF16)


def _attention(proj, sinks, first, last):
    t = proj.shape[0]
    nqb = ATT_QB // WINDOW
    nrb = t // WINDOW
    kcol = COL_K // HEAD_DIM
    vcol = COL_V // HEAD_DIM
    qspec = pl.BlockSpec((ATT_QB, GROUP * HEAD_DIM), lambda i, kv, *_: (i, kv))

    def band_specs(col):
        return [
            pl.BlockSpec((WINDOW, HEAD_DIM), lambda i, kv, *_: (jnp.maximum(i * nqb - 1, 0), col + kv)),
            pl.BlockSpec((ATT_QB, HEAD_DIM), lambda i, kv, *_: (i, col + kv)),
            pl.BlockSpec((WINDOW, HEAD_DIM), lambda i, kv, *_: (jnp.minimum((i + 1) * nqb, nrb - 1), col + kv)),
        ]

    grid_spec = pltpu.PrefetchScalarGridSpec(
        num_scalar_prefetch=3,
        grid=(t // ATT_QB, N_KV_HEADS),
        in_specs=[qspec] + band_specs(kcol) + band_specs(vcol),
        out_specs=pl.BlockSpec((ATT_QB, GROUP * HEAD_DIM), lambda i, kv, *_: (i, kv)),
    )
    return pl.pallas_call(
        _attn_kernel,
        grid_spec=grid_spec,
        out_shape=jax.ShapeDtypeStruct((t, Q_COLS), BF16),
        compiler_params=_params(("parallel", "arbitrary")),
        name="window_attn",
    )(first, last, sinks, proj, proj, proj, proj, proj, proj, proj)


def _mix_kernel(attn_ref, u0_ref, u1_ref, sv0_ref, sv1_ref, ga_ref, gs_ref, wa_ref, wb_ref, ws_ref, bs_ref,
                gsv_ref, o_ref, vn_ref, sg_ref):
    j = pl.program_id(1)
    half = SG_WIDTH // 2
    groups_per_half = half // SG_CHUNK

    def spatial_gating():
        sv0 = sv0_ref[...].astype(F32)
        sv1 = sv1_ref[...].astype(F32)
        ms = (jnp.sum(sv0 * sv0, axis=-1, keepdims=True) + jnp.sum(sv1 * sv1, axis=-1, keepdims=True)) / SG_WIDTH
        r = lax.rsqrt(ms + EPS)
        vn_ref[:, :half] = (sv0 * r * gsv_ref[:, :half]).astype(BF16)
        vn_ref[:, half:] = (sv1 * r * gsv_ref[:, half:]).astype(BF16)
        for c in range(MIX_TM // SG_CHUNK):
            rows = slice(c * SG_CHUNK, (c + 1) * SG_CHUNK)
            for g in range(SG_GROUPS):
                cols = slice(g * SG_CHUNK, (g + 1) * SG_CHUNK)
                u_ref = u0_ref if g < groups_per_half else u1_ref
                g_in = g % groups_per_half
                u = u_ref[rows, g_in * SG_CHUNK:(g_in + 1) * SG_CHUNK]
                mixed = jnp.dot(ws_ref[g], vn_ref[rows, cols], preferred_element_type=F32) + bs_ref[g]
                sg_ref[rows, cols] = (u.astype(F32) * mixed).astype(BF16)

    def project(first):
        pa = jnp.dot(attn_ref[...], wa_ref[...], preferred_element_type=F32)
        if first:
            spatial_gating()
        ps = jnp.dot(sg_ref[...], wb_ref[...], preferred_element_type=F32)
        merged = ga_ref[...].astype(F32) * pa + gs_ref[...].astype(F32) * ps
        o_ref[...] = merged.astype(BF16)

    @pl.when(j == 0)
    def _():
        project(True)

    @pl.when(j > 0)
    def _():
        project(False)


def _mix(attn, proj, wa, wb, ws, bs_b, gsv):
    t = attn.shape[0]
    grid = (t // MIX_TM, D_MODEL // MIX_TN)
    ga0 = COL_GA // MIX_TN
    gs0 = COL_GS // MIX_TN
    half = SG_WIDTH // 2
    u0 = COL_U // half
    sv0 = COL_SV // half
    return pl.pallas_call(
        _mix_kernel,
        grid=grid,
        in_specs=[
            pl.BlockSpec((MIX_TM, Q_COLS), lambda i, j: (i, 0)),
            pl.BlockSpec((MIX_TM, half), lambda i, j: (i, u0)),
            pl.BlockSpec((MIX_TM, half), lambda i, j: (i, u0 + 1)),
            pl.BlockSpec((MIX_TM, half), lambda i, j: (i, sv0)),
            pl.BlockSpec((MIX_TM, half), lambda i, j: (i, sv0 + 1)),
            pl.BlockSpec((MIX_TM, MIX_TN), lambda i, j: (i, ga0 + j)),
            pl.BlockSpec((MIX_TM, MIX_TN), lambda i, j: (i, gs0 + j)),
            pl.BlockSpec((Q_COLS, MIX_TN), lambda i, j: (0, j)),
            pl.BlockSpec((SG_WIDTH, MIX_TN), lambda i, j: (0, j)),
            pl.BlockSpec((SG_GROUPS, SG_CHUNK, SG_CHUNK), lambda i, j: (0, 0, 0)),
            pl.BlockSpec((SG_GROUPS, SG_CHUNK, SG_CHUNK), lambda i, j: (0, 0, 0)),
            pl.BlockSpec((1, SG_WIDTH), lambda i, j: (0, 0)),
        ],
        out_specs=pl.BlockSpec((MIX_TM, MIX_TN), lambda i, j: (i, j)),
        out_shape=jax.ShapeDtypeStruct((t, D_MODEL), BF16),
        scratch_shapes=[pltpu.VMEM((MIX_TM, SG_WIDTH), BF16), pltpu.VMEM((MIX_TM, SG_WIDTH), BF16)],
        compiler_params=_params(("parallel", "arbitrary")),
        name="branch_mix",
    )(attn, proj, proj, proj, proj, proj, proj, wa, wb, ws, bs_b, gsv)


def _out_proj_kernel(m_ref, w_ref, xp_ref, xs_ref, o_ref, *, n_prompt_blocks):
    i = pl.program_id(0)
    y = jnp.dot(m_ref[...], w_ref[...], preferred_element_type=F32)

    @pl.when(i < n_prompt_blocks)
    def _():
        o_ref[...] = xp_ref[...] + y

    @pl.when(i >= n_prompt_blocks)
    def _():
        o_ref[...] = xs_ref[...] + y


def _out_proj(merged, w_o, x_p, x_s):
    t = merged.shape[0]
    npb = x_p.shape[0] // OUT_TM
    nj = D_MODEL // OUT_TN
    return pl.pallas_call(
        functools.partial(_out_proj_kernel, n_prompt_blocks=npb),
        grid=(t // OUT_TM, nj),
        in_specs=[
            pl.BlockSpec((OUT_TM, D_MODEL), lambda i, j: (i, 0)),
            pl.BlockSpec((D_MODEL, OUT_TN), lambda i, j: (0, j)),
            pl.BlockSpec((OUT_TM, OUT_TN), lambda i, j: (jnp.minimum(i, npb - 1), jnp.where(i < npb, j, nj - 1))),
            pl.BlockSpec((OUT_TM, OUT_TN), lambda i, j: (jnp.maximum(i - npb, 0), jnp.where(i < npb, 0, j))),
        ],
        out_specs=pl.BlockSpec((OUT_TM, OUT_TN), lambda i, j: (i, j)),
        out_shape=jax.ShapeDtypeStruct((t, D_MODEL), F32),
        compiler_params=_params(("parallel", "arbitrary")),
        name="out_proj",
    )(merged, w_o, x_p, x_s)


def _split_bf16(a):
    hi = a.astype(BF16)
    lo = (a - hi.astype(F32)).astype(BF16)
    return hi, lo


def _router_kernel(x_ref, g_ref, wr_ref, br_ref, idx_ref, wgt_ref, rank_ref, cnt_ref, carry_ref):
    step = pl.program_id(0)

    @pl.when(step == 0)
    def _():
        carry_ref[...] = jnp.zeros_like(carry_ref)

    x = x_ref[...]
    ms = jnp.mean(x * x, axis=-1, keepdims=True)
    h = x * lax.rsqrt(ms + EPS) * g_ref[...]

    h_hi, h_lo = _split_bf16(h)
    w_hi, w_lo = _split_bf16(wr_ref[...])
    dn = (((1,), (1,)), ((), ()))
    logits = (lax.dot_general(w_hi, h_hi, dn, preferred_element_type=F32)
              + lax.dot_general(w_hi, h_lo, dn, preferred_element_type=F32)
              + lax.dot_general(w_lo, h_hi, dn, preferred_element_type=F32))
    logits = logits + br_ref[...]

    e_iota = lax.broadcasted_iota(jnp.int32, (N_EXPERTS, RT_TM), 0)
    vals = logits
    tops, idxs = [], []
    onehot = jnp.zeros((N_EXPERTS, RT_TM), F32)
    for _ in range(TOP_K):
        m = jnp.max(vals, axis=0, keepdims=True)
        idx = jnp.min(jnp.where(vals == m, e_iota, N_EXPERTS), axis=0, keepdims=True)
        sel = e_iota == idx
        onehot = onehot + sel.astype(F32)
        vals = jnp.where(sel, -jnp.inf, vals)
        tops.append(m)
        idxs.append(idx)

    exps = [jnp.exp(v - tops[0]) for v in tops]
    den = exps[0] + exps[1] + exps[2] + exps[3]

    tr = lax.broadcasted_iota(jnp.int32, (RT_TM, RT_TM), 0)
    tc = lax.broadcasted_iota(jnp.int32, (RT_TM, RT_TM), 1)
    upper = jnp.where(tr < tc, 1.0, 0.0).astype(BF16)
    before = jnp.dot(onehot.astype(BF16), upper, preferred_element_type=F32) + carry_ref[:, 0:1]

    for k in range(TOP_K):
        sel = e_iota == idxs[k]
        rank = jnp.sum(jnp.where(sel, before, 0.0), axis=0, keepdims=True)
        idx_ref[k:k + 1, :] = idxs[k]
        wgt_ref[k:k + 1, :] = exps[k] / den
        rank_ref[k:k + 1, :] = rank.astype(jnp.int32)

    carry_ref[...] = carry_ref[...] + jnp.sum(onehot, axis=1, keepdims=True)
    cnt_ref[...] = carry_ref[...]


def _router(x1, g_ffn, w_router_t, b_router):
    t = x1.shape[0]
    return pl.pallas_call(
        _router_kernel,
        grid=(t // RT_TM,),
        in_specs=[
            pl.BlockSpec((RT_TM, D_MODEL), lambda i: (i, 0)),
            pl.BlockSpec((1, D_MODEL), lambda i: (0, 0)),
            pl.BlockSpec((N_EXPERTS, D_MODEL), lambda i: (0, 0)),
            pl.BlockSpec((N_EXPERTS, 1), lambda i: (0, 0)),
        ],
        out_specs=[
            pl.BlockSpec((TOP_K, RT_TM), lambda i: (0, i)),
            pl.BlockSpec((TOP_K, RT_TM), lambda i: (0, i)),
            pl.BlockSpec((TOP_K, RT_TM), lambda i: (0, i)),
            pl.BlockSpec((N_EXPERTS, 128), lambda i: (0, 0)),
        ],
        out_shape=[
            jax.ShapeDtypeStruct((TOP_K, t), jnp.int32),
            jax.ShapeDtypeStruct((TOP_K, t), F32),
            jax.ShapeDtypeStruct((TOP_K, t), jnp.int32),
            jax.ShapeDtypeStruct((N_EXPERTS, 128), F32),
        ],
        scratch_shapes=[pltpu.VMEM((N_EXPERTS, 128), F32)],
        compiler_params=_params(("arbitrary",)),
        name="router",
    )(x1, g_ffn, w_router_t, b_router)


def _row_block_wait(src, dst, sem, copies):
    for _ in range(copies):
        pltpu.make_async_copy(src, dst, sem).wait()


def _dispatch_kernel(padstart_ref, padcnt_ref, dest_ref, x_ref, g_ref, xs_ref, hbuf, zrow, sems, zsem):
    i = pl.program_id(0)
    n = pl.num_programs(0)
    slot = i % 2

    def zero_copy(row):
        return pltpu.make_async_copy(zrow.at[pl.ds(0, 1), :], xs_ref.at[pl.ds(row, 1), :], zsem)

    @pl.when(i == 0)
    def _():
        zrow[...] = jnp.zeros_like(zrow)
        for e in range(N_EXPERTS):
            def start(r, c, e=e):
                zero_copy(padstart_ref[e] + r).start()
                return c
            lax.fori_loop(0, padcnt_ref[e], start, 0)

    x = x_ref[...]
    ms = jnp.mean(x * x, axis=-1, keepdims=True)
    h = x * lax.rsqrt(ms + EPS) * g_ref[...]
    hbuf[slot] = _pack_pair(h[:, :HALF], h[:, HALF:])

    def send(r, c):
        for k in range(TOP_K):
            d = dest_ref[0, 0, k * DSP_TB + r]
            pltpu.make_async_copy(hbuf.at[slot, pl.ds(r, 1), :], xs_ref.at[pl.ds(d, 1), :],
                                  sems.at[slot]).start(priority=k % 2)
        return c

    lax.fori_loop(0, DSP_TB, send, 0)

    def wait_slot(s):
        _row_block_wait(hbuf.at[s], xs_ref.at[pl.ds(0, DSP_TB), :], sems.at[s], TOP_K)

    @pl.when(i > 0)
    def _():
        wait_slot(1 - slot)

    @pl.when(i == n - 1)
    def _():
        wait_slot(slot)
        for e in range(N_EXPERTS):
            def done(r, c):
                zero_copy(0).wait()
                return c
            lax.fori_loop(0, padcnt_ref[e], done, 0)


def _dispatch(padstart, padcnt, dest_blocks, x1, g_ffn, n_rows):
    t = x1.shape[0]
    grid_spec = pltpu.PrefetchScalarGridSpec(
        num_scalar_prefetch=2,
        grid=(t // DSP_TB,),
        in_specs=[
            pl.BlockSpec((1, 1, TOP_K * DSP_TB), lambda i, *_: (i, 0, 0), memory_space=pltpu.SMEM),
            pl.BlockSpec((DSP_TB, D_MODEL), lambda i, *_: (i, 0)),
            pl.BlockSpec((1, D_MODEL), lambda i, *_: (0, 0)),
        ],
        out_specs=pl.BlockSpec(memory_space=pl.ANY),
        scratch_shapes=[
            pltpu.VMEM((2, DSP_TB, HALF), U32),
            pltpu.VMEM((8, HALF), U32),
            pltpu.SemaphoreType.DMA((2,)),
            pltpu.SemaphoreType.DMA(()),
        ],
    )
    return pl.pallas_call(
        _dispatch_kernel,
        grid_spec=grid_spec,
        out_shape=jax.ShapeDtypeStruct((n_rows, HALF), U32),
        compiler_params=_params(("arbitrary",)),
        name="moe_dispatch",
    )(padstart, padcnt, dest_blocks, x1, g_ffn)


def _moe_kernel(be_ref, ns_ref, na_ref, x_ref, wg_ref, wu_ref, bg_ref, bu_ref, wd_ref, bd_ref, o_ref,
                acc_ref, xb_ref):
    b = pl.program_id(0)
    j = pl.program_id(1)
    nj = pl.num_programs(1)
    nsub = ns_ref[b]

    n_full = MOE_BM // MOE_SUB
    full = nsub == n_full
    whole = slice(None)

    def rows_of(s):
        return pl.ds(pl.multiple_of(s * MOE_SUB, MOE_SUB), MOE_SUB)

    def for_rows(fn):
        @pl.when(full)
        def _():
            fn(whole, MOE_BM)

        @pl.when(jnp.logical_not(full))
        def _():
            def body(s, c):
                fn(rows_of(s), MOE_SUB)
                return c
            lax.fori_loop(0, nsub, body, 0)

    def unpack(rows, m):
        lo, hi = _unpack_pair(x_ref[rows, :])
        xb_ref[0, rows, :] = lo.astype(BF16)
        xb_ref[1, rows, :] = hi.astype(BF16)
        acc_ref[rows, :] = jnp.zeros((m, D_MODEL), F32)

    def expert_mlp(rows, m):
        wg = wg_ref[...].astype(BF16)
        wu = wu_ref[...].astype(BF16)
        x0 = xb_ref[0, rows, :]
        x1 = xb_ref[1, rows, :]
        gate = (jnp.dot(x0, wg[:HALF], preferred_element_type=F32)
                + jnp.dot(x1, wg[HALF:], preferred_element_type=F32) + bg_ref[...])
        up = (jnp.dot(x0, wu[:HALF], preferred_element_type=F32)
              + jnp.dot(x1, wu[HALF:], preferred_element_type=F32) + bu_ref[...])
        gate = jnp.minimum(gate, SWIGLU_LIMIT)
        up = jnp.clip(up, -SWIGLU_LIMIT, SWIGLU_LIMIT)
        act = (up + 1.0) * gate * jax.nn.sigmoid(SWIGLU_ALPHA * gate)
        acc_ref[rows, :] += jnp.dot(act.astype(BF16), wd_ref[...].astype(BF16), preferred_element_type=F32)

    def finish(rows, m):
        y = acc_ref[rows, :] + bd_ref[...]
        o_ref[rows, :] = _pack_pair(y[:, :HALF], y[:, HALF:])

    @pl.when(j == 0)
    def _():
        for_rows(unpack)

    for_rows(expert_mlp)

    @pl.when(j == nj - 1)
    def _():
        for_rows(finish)

        def blank(s, c):
            o_ref[rows_of(s), :] = jnp.zeros((MOE_SUB, HALF), U32)
            return c
        lax.fori_loop(nsub, n_full, blank, 0)


def _moe(block_e, block_nsub, n_active, xs, w_gu, b_gu, w_down, b_down):
    n_rows = xs.shape[0]
    nfc = D_FF // MOE_FC

    def frozen_j(b, j, na):
        return jnp.where(b < na[0], j, nfc - 1)

    grid_spec = pltpu.PrefetchScalarGridSpec(
        num_scalar_prefetch=3,
        grid=(n_rows // MOE_BM, nfc),
        in_specs=[
            pl.BlockSpec((MOE_BM, HALF), lambda b, j, be, ns, na: (jnp.minimum(b, na[0] - 1), 0)),
            pl.BlockSpec((None, D_MODEL, MOE_FC), lambda b, j, be, ns, na: (be[b], 0, frozen_j(b, j, na))),
            pl.BlockSpec((None, D_MODEL, MOE_FC), lambda b, j, be, ns, na: (be[b], 0, nfc + frozen_j(b, j, na))),
            pl.BlockSpec((None, 1, MOE_FC), lambda b, j, be, ns, na: (be[b], 0, frozen_j(b, j, na))),
            pl.BlockSpec((None, 1, MOE_FC), lambda b, j, be, ns, na: (be[b], 0, nfc + frozen_j(b, j, na))),
            pl.BlockSpec((None, MOE_FC, D_MODEL), lambda b, j, be, ns, na: (be[b], frozen_j(b, j, na), 0)),
            pl.BlockSpec((None, 1, D_MODEL), lambda b, j, be, ns, na: (be[b], 0, 0)),
        ],
        out_specs=pl.BlockSpec((MOE_BM, HALF), lambda b, j, be, ns, na: (b, 0)),
        scratch_shapes=[pltpu.VMEM((MOE_BM, D_MODEL), F32), pltpu.VMEM((2, MOE_BM, HALF), BF16)],
    )
    return pl.pallas_call(
        _moe_kernel,
        grid_spec=grid_spec,
        out_shape=jax.ShapeDtypeStruct((n_rows, HALF), U32),
        compiler_params=_params(("arbitrary", "arbitrary"), MOE_VMEM_LIMIT),
        name="moe_experts",
    )(block_e, block_nsub, n_active, xs, w_gu, w_gu, b_gu, b_gu, w_down, b_down)


def _combine_ple_kernel(dcur_ref, dnext_ref, x_ref, wt_ref, ys_ref, g_ref, pp_ref, ps_ref, wg_ref, wp_ref,
                        op_ref, os_ref, buf, sems, *, n_prompt_blocks):
    i = pl.program_id(0)
    n = pl.num_programs(0)
    slot = i % 2

    def fetch_row(dref, s, r):
        for k in range(TOP_K):
            d = dref[0, 0, k * CMB_TB + r]
            pltpu.make_async_copy(ys_ref.at[pl.ds(d, 1), :], buf.at[s, k, pl.ds(r, 1), :],
                                  sems.at[s]).start(priority=k % 2)

    def wait_slot(s):
        _row_block_wait(ys_ref.at[pl.ds(0, CMB_TB), :], buf.at[s, 0], sems.at[s], TOP_K)

    @pl.when(i == 0)
    def _():
        def body(r, c):
            fetch_row(dcur_ref, 0, r)
            return c
        lax.fori_loop(0, CMB_TB, body, 0)

    wait_slot(slot)

    groups = iter(np.array_split(np.arange(CMB_TB), PLE_CHUNKS))

    def issue_group():
        for row in next(groups):
            fetch_row(dnext_ref, 1 - slot, int(row))

    x_lo = x_ref[:, :HALF]
    x_hi = x_ref[:, HALF:]
    for k in range(TOP_K):
        lo, hi = _unpack_pair(buf[slot, k])
        w = wt_ref[:, k:k + 1]
        x_lo = x_lo + w * lo
        x_hi = x_hi + w * hi

    ms = (jnp.sum(x_lo * x_lo, axis=-1, keepdims=True) + jnp.sum(x_hi * x_hi, axis=-1, keepdims=True)) / D_MODEL
    r = lax.rsqrt(ms + EPS)
    h_lo = (x_lo * r * g_ref[:, :HALF]).astype(BF16)
    h_hi = (x_hi * r * g_ref[:, HALF:]).astype(BF16)

    cols_per_chunk = D_MODEL // PLE_CHUNKS
    gates = []
    for c in range(PLE_CHUNKS):
        issue_group()
        cols = slice(c * cols_per_chunk, (c + 1) * cols_per_chunk)
        gates.append(jax.nn.sigmoid(jnp.dot(h_lo, wg_ref[:HALF, cols], preferred_element_type=F32)
                                    + jnp.dot(h_hi, wg_ref[HALF:, cols], preferred_element_type=F32)))
    gate = jnp.concatenate(gates, axis=1)

    def finish(p_ref, o_ref):
        pp = jnp.dot(p_ref[...].astype(BF16), wp_ref[...], preferred_element_type=F32)
        o_ref[:, :HALF] = x_lo + gate[:, :HALF] * pp[:, :HALF]
        o_ref[:, HALF:] = x_hi + gate[:, HALF:] * pp[:, HALF:]

    @pl.when(i < n_prompt_blocks)
    def _():
        finish(pp_ref, op_ref)

    @pl.when(i >= n_prompt_blocks)
    def _():
        finish(ps_ref, os_ref)

    @pl.when(i == n - 1)
    def _():
        wait_slot(1 - slot)


def _combine_ple(dest_blocks, x1, wgt_t, ys, g_ple, p_p, p_s, w_gate, w_proj):
    t = x1.shape[0]
    nb = t // CMB_TB
    npb = p_p.shape[0] // CMB_TB
    const = lambda i: (0, 0)
    prompt_rows = lambda i: (jnp.minimum(i, npb - 1), 0)
    sample_rows = lambda i: (jnp.maximum(i - npb, 0), 0)
    return pl.pallas_call(
        functools.partial(_combine_ple_kernel, n_prompt_blocks=npb),
        grid=(nb,),
        in_specs=[
            pl.BlockSpec((1, 1, TOP_K * CMB_TB), lambda i: (i, 0, 0), memory_space=pltpu.SMEM),
            pl.BlockSpec((1, 1, TOP_K * CMB_TB), lambda i: (jnp.minimum(i + 1, nb - 1), 0, 0),
                         memory_space=pltpu.SMEM),
            pl.BlockSpec((CMB_TB, D_MODEL), lambda i: (i, 0)),
            pl.BlockSpec((CMB_TB, TOP_K), lambda i: (i, 0)),
            pl.BlockSpec(memory_space=pl.ANY),
            pl.BlockSpec((1, D_MODEL), const),
            pl.BlockSpec((CMB_TB, PLE_DIM), prompt_rows),
            pl.BlockSpec((CMB_TB, PLE_DIM), sample_rows),
            pl.BlockSpec((D_MODEL, D_MODEL), const),
            pl.BlockSpec((PLE_DIM, D_MODEL), const),
        ],
        out_specs=[
            pl.BlockSpec((CMB_TB, D_MODEL), prompt_rows),
            pl.BlockSpec((CMB_TB, D_MODEL), sample_rows),
        ],
        out_shape=[
            jax.ShapeDtypeStruct((p_p.shape[0], D_MODEL), F32),
            jax.ShapeDtypeStruct((p_s.shape[0], D_MODEL), F32),
        ],
        scratch_shapes=[
            pltpu.VMEM((2, TOP_K, CMB_TB, HALF), U32),
            pltpu.SemaphoreType.DMA((2,)),
        ],
        compiler_params=_params(("arbitrary",)),
        name="moe_combine_ple",
    )(dest_blocks, dest_blocks, x1, wgt_t, ys, g_ple, p_p, p_s, w_gate, w_proj)


def _rope_tables(max_len):
    half = HEAD_DIM // 2
    inv_freq = ROPE_THETA ** (-jnp.arange(half, dtype=F32) / half)
    ang = jnp.arange(max_len, dtype=F32)[:, None] * inv_freq[None, :]
    cos, sin = jnp.cos(ang), jnp.sin(ang)
    return jnp.concatenate([cos, cos], axis=-1), jnp.concatenate([-sin, sin], axis=-1)


def _head_matrices():
    col = np.arange(IN_NC)
    same_head = (col[:, None] // HEAD_DIM) == (col[None, :] // HEAD_DIM)
    head_sum = np.where(same_head, 1.0 / HEAD_DIM, 0.0)
    src = (col // HEAD_DIM) * HEAD_DIM + (col % HEAD_DIM + HEAD_DIM // 2) % HEAD_DIM
    head_rot = (col[:, None] == src[None, :]).astype(np.float32)
    return jnp.asarray(head_sum, BF16), jnp.asarray(head_rot, BF16)


def _seq_block_tables(seq_lens):
    pos, first, last = [], [], []
    for s in seq_lens:
        pos += list(range(s // IN_TM))
        nb = s // ATT_QB
        first += [1] + [0] * (nb - 1)
        last += [0] * (nb - 1) + [1]
    return jnp.array(pos, jnp.int32), jnp.array(first, jnp.int32), jnp.array(last, jnp.int32)


def _layer(x_p, x_s, p_p, p_s, seq_lens, g_mix, w_in, q_norm, k_norm, sinks, g_sg_v, w_s, b_s, w_branch, w_o,
           g_ffn, w_router, b_router, w_gu, b_gu, w_down, b_down, g_ple, w_ple_gate, w_ple_proj):
    t = x_p.shape[0] + x_s.shape[0]
    w_in_r = w_in.astype(BF16)
    cos_t, sin_t = _rope_tables(max(seq_lens))
    head_sum, head_rot = _head_matrices()
    pos_blk, first, last = _seq_block_tables(seq_lens)
    reps = IN_NC // HEAD_DIM
    q_gain = jnp.tile(q_norm * (HEAD_DIM ** -0.5 * LOG2E), reps)[None, :]
    k_gain = jnp.tile(k_norm, reps)[None, :]

    proj = _in_proj(pos_blk, x_p, x_s, g_mix[None, :], w_in_r, q_gain, k_gain, cos_t, sin_t, head_sum, head_rot)
    attn = _attention(proj, sinks * LOG2E, first, last)
    bs_b = jnp.broadcast_to(b_s[:, :, None], (SG_GROUPS, SG_CHUNK, SG_CHUNK))
    merged = _mix(attn, proj, w_branch[0].astype(BF16), w_branch[1].astype(BF16), w_s.astype(BF16), bs_b,
                  g_sg_v[None, :])
    x1 = _out_proj(merged, w_o.astype(BF16), x_p, x_s)

    idx, wgt, rank, cnt = _router(x1, g_ffn[None, :], w_router.T, b_router[:, None])

    counts = cnt[:, 0].astype(jnp.int32)
    padded = (counts + MOE_BM - 1) // MOE_BM * MOE_BM
    padded_end = jnp.cumsum(padded)
    padded_start = padded_end - padded
    experts = jnp.arange(N_EXPERTS, dtype=jnp.int32)
    dest = rank + jnp.sum(jnp.where(idx[:, :, None] == experts, padded_start, 0), axis=-1)
    n_blocks = (t * TOP_K) // MOE_BM + N_EXPERTS
    block_row = jnp.arange(n_blocks, dtype=jnp.int32) * MOE_BM
    n_active = padded_end[-1] // MOE_BM
    block_e = jnp.minimum(jnp.sum(padded_end[None, :] <= block_row[:, None], axis=1), N_EXPERTS - 1)
    block_valid = jnp.clip(padded_start[block_e] + counts[block_e] - block_row, 0, MOE_BM)
    block_nsub = ((block_valid + MOE_SUB - 1) // MOE_SUB).astype(jnp.int32)
    block_e = jnp.where(block_row < padded_end[-1], block_e, block_e[n_active - 1]).astype(jnp.int32)

    def token_blocks(tb):
        return dest.reshape(TOP_K, t // tb, tb).transpose(1, 0, 2).reshape(t // tb, 1, TOP_K * tb)

    xs = _dispatch(padded_start + counts, (-counts) % MOE_SUB, token_blocks(DSP_TB), x1, g_ffn[None, :],
                   n_blocks * MOE_BM)
    ys = _moe(block_e, block_nsub, n_active[None].astype(jnp.int32), xs, w_gu, b_gu[:, None, :], w_down,
              b_down[:, None, :])
    return _combine_ple(token_blocks(CMB_TB), x1, wgt.T, ys, g_ple[None, :], p_p, p_s,
                        w_ple_gate.astype(BF16), w_ple_proj.astype(BF16))


def kernel(x_prompt, x_sample, p_prompt, p_sample, g_mix, w_in, q_norm, k_norm, sinks, g_sg_v, w_s, b_s,
           w_branch, w_o, g_ffn, w_router, b_router, w_gu, b_gu, w_down, b_down, g_ple, w_ple_gate,
           w_ple_proj):
    depth = g_mix.shape[0]
    bp, sp, d = x_prompt.shape
    bs, ss, _ = x_sample.shape
    seq_lens = [sp] * bp + [ss] * bs
    x_p = x_prompt.reshape(bp * sp, d)
    x_s = x_sample.reshape(bs * ss, d)
    for l in range(depth):
        x_p, x_s = _layer(x_p, x_s, p_prompt[l].reshape(bp * sp, PLE_DIM), p_sample[l].reshape(bs * ss, PLE_DIM),
                          seq_lens, g_mix[l], w_in[l], q_norm[l], k_norm[l], sinks[l], g_sg_v[l], w_s[l], b_s[l],
                          w_branch[l], w_o[l], g_ffn[l], w_router[l], b_router[l], w_gu[l], b_gu[l], w_down[l],
                          b_down[l], g_ple[l], w_ple_gate[l], w_ple_proj[l])
    return x_p.reshape(bp, sp, d), x_s.reshape(bs, ss, d)
```

```python
import functools
import math

import jax
import jax.numpy as jnp
import numpy as np
from jax import lax
from jax.experimental import pallas as pl
from jax.experimental.pallas import tpu as pltpu

F32 = jnp.float32
BF16 = jnp.bfloat16
U32 = jnp.uint32

D_MODEL = 2048
HALF = D_MODEL // 2
HEAD_DIM = 128
N_HEADS = 16
N_KV_HEADS = 4
GROUP = N_HEADS // N_KV_HEADS
WINDOW = 128
ROPE_THETA = 10000.0
SG_GROUPS = 16
SG_WIDTH = 2048
SG_CHUNK = 128
N_EXPERTS = 32
TOP_K = 4
D_FF = 2048
SWIGLU_LIMIT = 7.0
SWIGLU_ALPHA = 1.702
PLE_DIM = 256
EPS = 1e-6
NEG_INF = -1e30
LOG2E = math.log2(math.e)

Q_COLS = N_HEADS * HEAD_DIM
KV_COLS = N_KV_HEADS * HEAD_DIM
IN_COLS = Q_COLS + 2 * KV_COLS + 2 * SG_WIDTH + 2 * D_MODEL

COL_Q = 0
COL_K = Q_COLS
COL_V = COL_K + KV_COLS
COL_U = COL_V + KV_COLS
COL_SV = COL_U + SG_WIDTH
COL_GA = COL_SV + SG_WIDTH
COL_GS = COL_GA + D_MODEL

VMEM_LIMIT = 56 * 1024 * 1024
MOE_VMEM_LIMIT = 62 * 1024 * 1024

IN_TM, IN_TN, IN_NC = 1024, 1024, 256
ATT_QB = 1024
MIX_TM, MIX_TN = 512, 1024
OUT_TM, OUT_TN = 1024, 1024
RT_TM = 512
MOE_BM, MOE_FC, MOE_SUB = 1024, 512, 256
DSP_TB = 256
CMB_TB = 256
PLE_CHUNKS = 8


def _params(sem, vmem_limit=VMEM_LIMIT):
    return pltpu.CompilerParams(dimension_semantics=sem, vmem_limit_bytes=vmem_limit)


def _pack_pair(lo, hi):
    def rounded_bits(x):
        return lax.bitcast_convert_type(x.astype(BF16).astype(F32), U32)
    return (rounded_bits(lo) >> 16) | rounded_bits(hi)


def _unpack_pair(u):
    lo = lax.bitcast_convert_type(u << 16, F32)
    hi = lax.bitcast_convert_type(u & U32(0xFFFF0000), F32)
    return lo, hi


def _in_proj_kernel(pos_ref, xp_ref, xs_ref, g_ref, w_ref, qg_ref, kg_ref, cos_ref, sin_ref, hsum_ref, rot_ref,
                    o_ref, hn_ref, *, n_prompt_blocks):
    i = pl.program_id(0)
    j = pl.program_id(1)

    def normalize(x_ref):
        x = x_ref[...]
        ms = jnp.mean(x * x, axis=-1, keepdims=True)
        hn_ref[...] = (x * lax.rsqrt(ms + EPS) * g_ref[...]).astype(BF16)

    @pl.when(jnp.logical_and(j == 0, i < n_prompt_blocks))
    def _():
        normalize(xp_ref)

    @pl.when(jnp.logical_and(j == 0, i >= n_prompt_blocks))
    def _():
        normalize(xs_ref)

    def rope_heads(gain):
        def epilogue(acc, col0):
            reps = IN_NC // HEAD_DIM
            cos = jnp.concatenate([cos_ref[...]] * reps, axis=1)
            sin = jnp.concatenate([sin_ref[...]] * reps, axis=1)
            ms = jnp.dot((acc * acc).astype(BF16), hsum_ref[...], preferred_element_type=F32)
            an = acc * lax.rsqrt(ms + EPS) * gain
            rot = jnp.dot(an.astype(BF16), rot_ref[...], preferred_element_type=F32)
            o_ref[:, col0:col0 + IN_NC] = (an * cos + rot * sin).astype(BF16)
        return epilogue

    def plain(fn):
        def epilogue(acc, col0):
            o_ref[:, col0:col0 + IN_NC] = fn(acc).astype(BF16)
        return epilogue

    def tiles(epilogues):
        def chunk_dot(c):
            return jnp.dot(hn_ref[...], w_ref[:, c * IN_NC:(c + 1) * IN_NC], preferred_element_type=F32)

        acc = chunk_dot(0)
        for c, epilogue in enumerate(epilogues):
            nxt = chunk_dot(c + 1) if c + 1 < len(epilogues) else None
            epilogue(acc, c * IN_NC)
            acc = nxt

    n_chunks = IN_TN // IN_NC
    jkv = COL_K // IN_TN
    ju = COL_U // IN_TN
    jg = COL_GA // IN_TN

    @pl.when(j < jkv)
    def _():
        tiles([rope_heads(qg_ref[...])] * n_chunks)

    @pl.when(j == jkv)
    def _():
        nk = KV_COLS // IN_NC
        tiles([rope_heads(kg_ref[...])] * nk + [plain(lambda a: a)] * (n_chunks - nk))

    @pl.when(jnp.logical_and(j >= ju, j < jg))
    def _():
        tiles([plain(jax.nn.gelu)] * n_chunks)

    @pl.when(j >= jg)
    def _():
        tiles([plain(jax.nn.sigmoid)] * n_chunks)


def _in_proj(pos_blk, x_p, x_s, g_mix, w_in_r, q_gain, k_gain, cos_t, sin_t, head_sum, head_rot):
    npb = x_p.shape[0] // IN_TM
    t = x_p.shape[0] + x_s.shape[0]
    const = lambda i, j, pb: (0, 0)
    grid_spec = pltpu.PrefetchScalarGridSpec(
        num_scalar_prefetch=1,
        grid=(t // IN_TM, IN_COLS // IN_TN),
        in_specs=[
            pl.BlockSpec((IN_TM, D_MODEL), lambda i, j, pb: (jnp.minimum(i, npb - 1), 0)),
            pl.BlockSpec((IN_TM, D_MODEL), lambda i, j, pb: (jnp.maximum(i - npb, 0), 0)),
            pl.BlockSpec((1, D_MODEL), const),
            pl.BlockSpec((D_MODEL, IN_TN), lambda i, j, pb: (0, j)),
            pl.BlockSpec((1, IN_NC), const),
            pl.BlockSpec((1, IN_NC), const),
            pl.BlockSpec((IN_TM, HEAD_DIM), lambda i, j, pb: (pb[i], 0)),
            pl.BlockSpec((IN_TM, HEAD_DIM), lambda i, j, pb: (pb[i], 0)),
            pl.BlockSpec((IN_NC, IN_NC), const),
            pl.BlockSpec((IN_NC, IN_NC), const),
        ],
        out_specs=pl.BlockSpec((IN_TM, IN_TN), lambda i, j, pb: (i, j)),
        scratch_shapes=[pltpu.VMEM((IN_TM, D_MODEL), BF16)],
    )
    return pl.pallas_call(
        functools.partial(_in_proj_kernel, n_prompt_blocks=npb),
        grid_spec=grid_spec,
        out_shape=jax.ShapeDtypeStruct((t, IN_COLS), BF16),
        compiler_params=_params(("parallel", "arbitrary")),
        name="in_proj",
    )(pos_blk, x_p, x_s, g_mix, w_in_r, q_gain, k_gain, cos_t, sin_t, head_sum, head_rot)


def _attn_kernel(first_ref, last_ref, sinks_ref, q_ref, kp_ref, km_ref, kn_ref,
                 vp_ref, vm_ref, vn_ref, o_ref):
    i = pl.program_id(0)
    kv = pl.program_id(1)
    nqb = ATT_QB // WINDOW
    rows = GROUP * WINDOW
    kband = jnp.concatenate([kp_ref[...], km_ref[...], kn_ref[...]], axis=0)
    vband = jnp.concatenate([vp_ref[...], vm_ref[...], vn_ref[...]], axis=0)
    r = lax.broadcasted_iota(jnp.int32, (rows, 3 * WINDOW), 0) & (WINDOW - 1)
    c = lax.broadcasted_iota(jnp.int32, (rows, 3 * WINDOW), 1)
    d = c - r
    band_ok = jnp.logical_and(d >= 0, d <= 2 * WINDOW)
    lo = jnp.where(first_ref[i] != 0, WINDOW, 0)
    hi = jnp.where(last_ref[i] != 0, 2 * WINDOW, 3 * WINDOW)
    head = lax.broadcasted_iota(jnp.int32, (rows, 1), 0) >> (WINDOW.bit_length() - 1)
    sink = jnp.zeros((rows, 1), F32)
    for g in range(GROUP):
        sink = jnp.where(head == g, sinks_ref[kv * GROUP + g], sink)
    def scores(b):
        kb = kband[b * WINDOW:(b + 3) * WINDOW]
        qrows = slice(b * WINDOW, (b + 1) * WINDOW)
        qs = jnp.concatenate([q_ref[qrows, g * HEAD_DIM:(g + 1) * HEAD_DIM] for g in range(GROUP)], axis=0)
        return lax.dot_general(qs, kb, (((1,), (1,)), ((), ())), preferred_element_type=F32)

    s_next = scores(0)
    for b in range(nqb):
        ok = band_ok
        if b == 0:
            ok = jnp.logical_and(ok, c >= lo)
        if b == nqb - 1:
            ok = jnp.logical_and(ok, c < hi)
        vb = vband[b * WINDOW:(b + 3) * WINDOW]
        qrows = slice(b * WINDOW, (b + 1) * WINDOW)
        s = s_next
        if b + 1 < nqb:
            s_next = scores(b + 1)
        s = jnp.where(ok, s, NEG_INF)
        m = jnp.maximum(jnp.max(s, axis=-1, keepdims=True), sink)
        p = jnp.exp2(s - m)
        den = jnp.sum(p, axis=-1, keepdims=True) + jnp.exp2(sink - m)
        o = jnp.dot(p.astype(BF16), vb, preferred_element_type=F32) / den
        for g in range(GROUP):
            o_ref[qrows, g * HEAD_DIM:(g + 1) * HEAD_DIM] = o[g * WINDOW:(g + 1) * WINDOW].astype(BF16)


def _attention(proj, sinks, first, last):
    t = proj.shape[0]
    nqb = ATT_QB // WINDOW
    nrb = t // WINDOW
    kcol = COL_K // HEAD_DIM
    vcol = COL_V // HEAD_DIM
    qspec = pl.BlockSpec((ATT_QB, GROUP * HEAD_DIM), lambda i, kv, *_: (i, kv))

    def band_specs(col):
        return [
            pl.BlockSpec((WINDOW, HEAD_DIM), lambda i, kv, *_: (jnp.maximum(i * nqb - 1, 0), col + kv)),
            pl.BlockSpec((ATT_QB, HEAD_DIM), lambda i, kv, *_: (i, col + kv)),
            pl.BlockSpec((WINDOW, HEAD_DIM), lambda i, kv, *_: (jnp.minimum((i + 1) * nqb, nrb - 1), col + kv)),
        ]

    grid_spec = pltpu.PrefetchScalarGridSpec(
        num_scalar_prefetch=3,
        grid=(t // ATT_QB, N_KV_HEADS),
        in_specs=[qspec] + band_specs(kcol) + band_specs(vcol),
        out_specs=pl.BlockSpec((ATT_QB, GROUP * HEAD_DIM), lambda i, kv, *_: (i, kv)),
    )
    return pl.pallas_call(
        _attn_kernel,
        grid_spec=grid_spec,
        out_shape=jax.ShapeDtypeStruct((t, Q_COLS), BF16),
        compiler_params=_params(("parallel", "arbitrary")),
        name="window_attn",
    )(first, last, sinks, proj, proj, proj, proj, proj, proj, proj)


def _mix_kernel(attn_ref, u0_ref, u1_ref, sv0_ref, sv1_ref, ga_ref, gs_ref, wa_ref, wb_ref, ws_ref, bs_ref,
                gsv_ref, o_ref, vn_ref, sg_ref):
    j = pl.program_id(1)
    half = SG_WIDTH // 2
    groups_per_half = half // SG_CHUNK

    def spatial_gating():
        sv0 = sv0_ref[...].astype(F32)
        sv1 = sv1_ref[...].astype(F32)
        ms = (jnp.sum(sv0 * sv0, axis=-1, keepdims=True) + jnp.sum(sv1 * sv1, axis=-1, keepdims=True)) / SG_WIDTH
        r = lax.rsqrt(ms + EPS)
        vn_ref[:, :half] = (sv0 * r * gsv_ref[:, :half]).astype(BF16)
        vn_ref[:, half:] = (sv1 * r * gsv_ref[:, half:]).astype(BF16)
        for c in range(MIX_TM // SG_CHUNK):
            rows = slice(c * SG_CHUNK, (c + 1) * SG_CHUNK)
            for g in range(SG_GROUPS):
                cols = slice(g * SG_CHUNK, (g + 1) * SG_CHUNK)
                u_ref = u0_ref if g < groups_per_half else u1_ref
                g_in = g % groups_per_half
                u = u_ref[rows, g_in * SG_CHUNK:(g_in + 1) * SG_CHUNK]
                mixed = jnp.dot(ws_ref[g], vn_ref[rows, cols], preferred_element_type=F32) + bs_ref[g]
                sg_ref[rows, cols] = (u.astype(F32) * mixed).astype(BF16)

    def project(first):
        pa = jnp.dot(attn_ref[...], wa_ref[...], preferred_element_type=F32)
        if first:
            spatial_gating()
        ps = jnp.dot(sg_ref[...], wb_ref[...], preferred_element_type=F32)
        merged = ga_ref[...].astype(F32) * pa + gs_ref[...].astype(F32) * ps
        o_ref[...] = merged.astype(BF16)

    @pl.when(j == 0)
    def _():
        project(True)

    @pl.when(j > 0)
    def _():
        project(False)


def _mix(attn, proj, wa, wb, ws, bs_b, gsv):
    t = attn.shape[0]
    grid = (t // MIX_TM, D_MODEL // MIX_TN)
    ga0 = COL_GA // MIX_TN
    gs0 = COL_GS // MIX_TN
    half = SG_WIDTH // 2
    u0 = COL_U // half
    sv0 = COL_SV // half
    return pl.pallas_call(
        _mix_kernel,
        grid=grid,
        in_specs=[
            pl.BlockSpec((MIX_TM, Q_COLS), lambda i, j: (i, 0)),
            pl.BlockSpec((MIX_TM, half), lambda i, j: (i, u0)),
            pl.BlockSpec((MIX_TM, half), lambda i, j: (i, u0 + 1)),
            pl.BlockSpec((MIX_TM, half), lambda i, j: (i, sv0)),
            pl.BlockSpec((MIX_TM, half), lambda i, j: (i, sv0 + 1)),
            pl.BlockSpec((MIX_TM, MIX_TN), lambda i, j: (i, ga0 + j)),
            pl.BlockSpec((MIX_TM, MIX_TN), lambda i, j: (i, gs0 + j)),
            pl.BlockSpec((Q_COLS, MIX_TN), lambda i, j: (0, j)),
            pl.BlockSpec((SG_WIDTH, MIX_TN), lambda i, j: (0, j)),
            pl.BlockSpec((SG_GROUPS, SG_CHUNK, SG_CHUNK), lambda i, j: (0, 0, 0)),
            pl.BlockSpec((SG_GROUPS, SG_CHUNK, SG_CHUNK), lambda i, j: (0, 0, 0)),
            pl.BlockSpec((1, SG_WIDTH), lambda i, j: (0, 0)),
        ],
        out_specs=pl.BlockSpec((MIX_TM, MIX_TN), lambda i, j: (i, j)),
        out_shape=jax.ShapeDtypeStruct((t, D_MODEL), BF16),
        scratch_shapes=[pltpu.VMEM((MIX_TM, SG_WIDTH), BF16), pltpu.VMEM((MIX_TM, SG_WIDTH), BF16)],
        compiler_params=_params(("parallel", "arbitrary")),
        name="branch_mix",
    )(attn, proj, proj, proj, proj, proj, proj, wa, wb, ws, bs_b, gsv)


def _out_proj_kernel(m_ref, w_ref, xp_ref, xs_ref, o_ref, *, n_prompt_blocks):
    i = pl.program_id(0)
    y = jnp.dot(m_ref[...], w_ref[...], preferred_element_type=F32)

    @pl.when(i < n_prompt_blocks)
    def _():
        o_ref[...] = xp_ref[...] + y

    @pl.when(i >= n_prompt_blocks)
    def _():
        o_ref[...] = xs_ref[...] + y


def _out_proj(merged, w_o, x_p, x_s):
    t = merged.shape[0]
    npb = x_p.shape[0] // OUT_TM
    nj = D_MODEL // OUT_TN
    return pl.pallas_call(
        functools.partial(_out_proj_kernel, n_prompt_blocks=npb),
        grid=(t // OUT_TM, nj),
        in_specs=[
            pl.BlockSpec((OUT_TM, D_MODEL), lambda i, j: (i, 0)),
            pl.BlockSpec((D_MODEL, OUT_TN), lambda i, j: (0, j)),
            pl.BlockSpec((OUT_TM, OUT_TN), lambda i, j: (jnp.minimum(i, npb - 1), jnp.where(i < npb, j, nj - 1))),
            pl.BlockSpec((OUT_TM, OUT_TN), lambda i, j: (jnp.maximum(i - npb, 0), jnp.where(i < npb, 0, j))),
        ],
        out_specs=pl.BlockSpec((OUT_TM, OUT_TN), lambda i, j: (i, j)),
        out_shape=jax.ShapeDtypeStruct((t, D_MODEL), F32),
        compiler_params=_params(("parallel", "arbitrary")),
        name="out_proj",
    )(merged, w_o, x_p, x_s)


def _split_bf16(a):
    hi = a.astype(BF16)
    lo = (a - hi.astype(F32)).astype(BF16)
    return hi, lo


def _router_kernel(x_ref, g_ref, wr_ref, br_ref, idx_ref, wgt_ref, rank_ref, cnt_ref, carry_ref):
    step = pl.program_id(0)

    @pl.when(step == 0)
    def _():
        carry_ref[...] = jnp.zeros_like(carry_ref)

    x = x_ref[...]
    ms = jnp.mean(x * x, axis=-1, keepdims=True)
    h = x * lax.rsqrt(ms + EPS) * g_ref[...]

    h_hi, h_lo = _split_bf16(h)
    w_hi, w_lo = _split_bf16(wr_ref[...])
    dn = (((1,), (1,)), ((), ()))
    logits = (lax.dot_general(w_hi, h_hi, dn, preferred_element_type=F32)
              + lax.dot_general(w_hi, h_lo, dn, preferred_element_type=F32)
              + lax.dot_general(w_lo, h_hi, dn, preferred_element_type=F32))
    logits = logits + br_ref[...]

    e_iota = lax.broadcasted_iota(jnp.int32, (N_EXPERTS, RT_TM), 0)
    vals = logits
    tops, idxs = [], []
    onehot = jnp.zeros((N_EXPERTS, RT_TM), F32)
    for _ in range(TOP_K):
        m = jnp.max(vals, axis=0, keepdims=True)
        idx = jnp.min(jnp.where(vals == m, e_iota, N_EXPERTS), axis=0, keepdims=True)
        sel = e_iota == idx
        onehot = onehot + sel.astype(F32)
        vals = jnp.where(sel, -jnp.inf, vals)
        tops.append(m)
        idxs.append(idx)

    exps = [jnp.exp(v - tops[0]) for v in tops]
    den = exps[0] + exps[1] + exps[2] + exps[3]

    tr = lax.broadcasted_iota(jnp.int32, (RT_TM, RT_TM), 0)
    tc = lax.broadcasted_iota(jnp.int32, (RT_TM, RT_TM), 1)
    upper = jnp.where(tr < tc, 1.0, 0.0).astype(BF16)
    before = jnp.dot(onehot.astype(BF16), upper, preferred_element_type=F32) + carry_ref[:, 0:1]

    for k in range(TOP_K):
        sel = e_iota == idxs[k]
        rank = jnp.sum(jnp.where(sel, before, 0.0), axis=0, keepdims=True)
        idx_ref[k:k + 1, :] = idxs[k]
        wgt_ref[k:k + 1, :] = exps[k] / den
        rank_ref[k:k + 1, :] = rank.astype(jnp.int32)

    carry_ref[...] = carry_ref[...] + jnp.sum(onehot, axis=1, keepdims=True)
    cnt_ref[...] = carry_ref[...]


def _router(x1, g_ffn, w_router_t, b_router):
    t = x1.shape[0]
    return pl.pallas_call(
        _router_kernel,
        grid=(t // RT_TM,),
        in_specs=[
            pl.BlockSpec((RT_TM, D_MODEL), lambda i: (i, 0)),
            pl.BlockSpec((1, D_MODEL), lambda i: (0, 0)),
            pl.BlockSpec((N_EXPERTS, D_MODEL), lambda i: (0, 0)),
            pl.BlockSpec((N_EXPERTS, 1), lambda i: (0, 0)),
        ],
        out_specs=[
            pl.BlockSpec((TOP_K, RT_TM), lambda i: (0, i)),
            pl.BlockSpec((TOP_K, RT_TM), lambda i: (0, i)),
            pl.BlockSpec((TOP_K, RT_TM), lambda i: (0, i)),
            pl.BlockSpec((N_EXPERTS, 128), lambda i: (0, 0)),
        ],
        out_shape=[
            jax.ShapeDtypeStruct((TOP_K, t), jnp.int32),
            jax.ShapeDtypeStruct((TOP_K, t), F32),
            jax.ShapeDtypeStruct((TOP_K, t), jnp.int32),
            jax.ShapeDtypeStruct((N_EXPERTS, 128), F32),
        ],
        scratch_shapes=[pltpu.VMEM((N_EXPERTS, 128), F32)],
        compiler_params=_params(("arbitrary",)),
        name="router",
    )(x1, g_ffn, w_router_t, b_router)


def _row_block_wait(src, dst, sem, copies):
    for _ in range(copies):
        pltpu.make_async_copy(src, dst, sem).wait()


def _dispatch_kernel(padstart_ref, padcnt_ref, dest_ref, x_ref, g_ref, xs_ref, hbuf, zrow, sems, zsem):
    i = pl.program_id(0)
    n = pl.num_programs(0)
    slot = i % 2

    def zero_copy(row):
        return pltpu.make_async_copy(zrow.at[pl.ds(0, 1), :], xs_ref.at[pl.ds(row, 1), :], zsem)

    @pl.when(i == 0)
    def _():
        zrow[...] = jnp.zeros_like(zrow)
        for e in range(N_EXPERTS):
            def start(r, c, e=e):
                zero_copy(padstart_ref[e] + r).start()
                return c
            lax.fori_loop(0, padcnt_ref[e], start, 0)

    x = x_ref[...]
    ms = jnp.mean(x * x, axis=-1, keepdims=True)
    h = x * lax.rsqrt(ms + EPS) * g_ref[...]
    hbuf[slot] = _pack_pair(h[:, :HALF], h[:, HALF:])

    def send(r, c):
        for k in range(TOP_K):
            d = dest_ref[0, 0, k * DSP_TB + r]
            pltpu.make_async_copy(hbuf.at[slot, pl.ds(r, 1), :], xs_ref.at[pl.ds(d, 1), :],
                                  sems.at[slot]).start(priority=k % 2)
        return c

    lax.fori_loop(0, DSP_TB, send, 0)

    def wait_slot(s):
        _row_block_wait(hbuf.at[s], xs_ref.at[pl.ds(0, DSP_TB), :], sems.at[s], TOP_K)

    @pl.when(i > 0)
    def _():
        wait_slot(1 - slot)

    @pl.when(i == n - 1)
    def _():
        wait_slot(slot)
        for e in range(N_EXPERTS):
            def done(r, c):
                zero_copy(0).wait()
                return c
            lax.fori_loop(0, padcnt_ref[e], done, 0)


def _dispatch(padstart, padcnt, dest_blocks, x1, g_ffn, n_rows):
    t = x1.shape[0]
    grid_spec = pltpu.PrefetchScalarGridSpec(
        num_scalar_prefetch=2,
        grid=(t // DSP_TB,),
        in_specs=[
            pl.BlockSpec((1, 1, TOP_K * DSP_TB), lambda i, *_: (i, 0, 0), memory_space=pltpu.SMEM),
            pl.BlockSpec((DSP_TB, D_MODEL), lambda i, *_: (i, 0)),
            pl.BlockSpec((1, D_MODEL), lambda i, *_: (0, 0)),
        ],
        out_specs=pl.BlockSpec(memory_space=pl.ANY),
        scratch_shapes=[
            pltpu.VMEM((2, DSP_TB, HALF), U32),
            pltpu.VMEM((8, HALF), U32),
            pltpu.SemaphoreType.DMA((2,)),
            pltpu.SemaphoreType.DMA(()),
        ],
    )
    return pl.pallas_call(
        _dispatch_kernel,
        grid_spec=grid_spec,
        out_shape=jax.ShapeDtypeStruct((n_rows, HALF), U32),
        compiler_params=_params(("arbitrary",)),
        name="moe_dispatch",
    )(padstart, padcnt, dest_blocks, x1, g_ffn)


def _moe_kernel(be_ref, ns_ref, na_ref, x_ref, wg_ref, wu_ref, bg_ref, bu_ref, wd_ref, bd_ref, o_ref,
                acc_ref, xb_ref):
    b = pl.program_id(0)
    j = pl.program_id(1)
    nj = pl.num_programs(1)
    nsub = ns_ref[b]

    n_full = MOE_BM // MOE_SUB
    full = nsub == n_full
    whole = slice(None)

    def rows_of(s):
        return pl.ds(pl.multiple_of(s * MOE_SUB, MOE_SUB), MOE_SUB)

    def for_rows(fn):
        for n in range(1, n_full + 1):
            @pl.when(nsub == n)
            def _(n=n):
                fn(whole if n == n_full else slice(0, n * MOE_SUB), n * MOE_SUB)

    def unpack(rows, m):
        lo, hi = _unpack_pair(x_ref[rows, :])
        xb_ref[0, rows, :] = lo.astype(BF16)
        xb_ref[1, rows, :] = hi.astype(BF16)
        acc_ref[rows, :] = jnp.zeros((m, D_MODEL), F32)

    def expert_mlp(rows, m):
        wg = wg_ref[...].astype(BF16)
        wu = wu_ref[...].astype(BF16)
        x0 = xb_ref[0, rows, :]
        x1 = xb_ref[1, rows, :]
        gate = (jnp.dot(x0, wg[:HALF], preferred_element_type=F32)
                + jnp.dot(x1, wg[HALF:], preferred_element_type=F32) + bg_ref[...])
        up = (jnp.dot(x0, wu[:HALF], preferred_element_type=F32)
              + jnp.dot(x1, wu[HALF:], preferred_element_type=F32) + bu_ref[...])
        gate = jnp.minimum(gate, SWIGLU_LIMIT)
        up = jnp.clip(up, -SWIGLU_LIMIT, SWIGLU_LIMIT)
        act = (up + 1.0) * gate * jax.nn.sigmoid(SWIGLU_ALPHA * gate)
        acc_ref[rows, :] += jnp.dot(act.astype(BF16), wd_ref[...].astype(BF16), preferred_element_type=F32)

    def finish(rows, m):
        y = acc_ref[rows, :] + bd_ref[...]
        o_ref[rows, :] = _pack_pair(y[:, :HALF], y[:, HALF:])

    @pl.when(j == 0)
    def _():
        for_rows(unpack)

    for_rows(expert_mlp)

    @pl.when(j == nj - 1)
    def _():
        for_rows(finish)

        def blank(s, c):
            o_ref[rows_of(s), :] = jnp.zeros((MOE_SUB, HALF), U32)
            return c
        lax.fori_loop(nsub, n_full, blank, 0)


def _moe(block_e, block_nsub, n_active, xs, w_gu, b_gu, w_down, b_down):
    n_rows = xs.shape[0]
    nfc = D_FF // MOE_FC

    def frozen_j(b, j, na):
        return jnp.where(b < na[0], j, nfc - 1)

    grid_spec = pltpu.PrefetchScalarGridSpec(
        num_scalar_prefetch=3,
        grid=(n_rows // MOE_BM, nfc),
        in_specs=[
            pl.BlockSpec((MOE_BM, HALF), lambda b, j, be, ns, na: (jnp.minimum(b, na[0] - 1), 0)),
            pl.BlockSpec((None, D_MODEL, MOE_FC), lambda b, j, be, ns, na: (be[b], 0, frozen_j(b, j, na))),
            pl.BlockSpec((None, D_MODEL, MOE_FC), lambda b, j, be, ns, na: (be[b], 0, nfc + frozen_j(b, j, na))),
            pl.BlockSpec((None, 1, MOE_FC), lambda b, j, be, ns, na: (be[b], 0, frozen_j(b, j, na))),
            pl.BlockSpec((None, 1, MOE_FC), lambda b, j, be, ns, na: (be[b], 0, nfc + frozen_j(b, j, na))),
            pl.BlockSpec((None, MOE_FC, D_MODEL), lambda b, j, be, ns, na: (be[b], frozen_j(b, j, na), 0)),
            pl.BlockSpec((None, 1, D_MODEL), lambda b, j, be, ns, na: (be[b], 0, 0)),
        ],
        out_specs=pl.BlockSpec((MOE_BM, HALF), lambda b, j, be, ns, na: (b, 0)),
        scratch_shapes=[pltpu.VMEM((MOE_BM, D_MODEL), F32), pltpu.VMEM((2, MOE_BM, HALF), BF16)],
    )
    return pl.pallas_call(
        _moe_kernel,
        grid_spec=grid_spec,
        out_shape=jax.ShapeDtypeStruct((n_rows, HALF), U32),
        compiler_params=_params(("arbitrary", "arbitrary"), MOE_VMEM_LIMIT),
        name="moe_experts",
    )(block_e, block_nsub, n_active, xs, w_gu, w_gu, b_gu, b_gu, w_down, b_down)


def _combine_ple_kernel(dcur_ref, dnext_ref, x_ref, wt_ref, ys_ref, g_ref, pp_ref, ps_ref, wg_ref, wp_ref,
                        op_ref, os_ref, buf, sems, *, n_prompt_blocks):
    i = pl.program_id(0)
    n = pl.num_programs(0)
    slot = i % 2

    def fetch_row(dref, s, r):
        for k in range(TOP_K):
            d = dref[0, 0, k * CMB_TB + r]
            pltpu.make_async_copy(ys_ref.at[pl.ds(d, 1), :], buf.at[s, k, pl.ds(r, 1), :],
                                  sems.at[s]).start(priority=k % 2)

    def wait_slot(s):
        _row_block_wait(ys_ref.at[pl.ds(0, CMB_TB), :], buf.at[s, 0], sems.at[s], TOP_K)

    @pl.when(i == 0)
    def _():
        def body(r, c):
            fetch_row(dcur_ref, 0, r)
            return c
        lax.fori_loop(0, CMB_TB, body, 0)

    wait_slot(slot)

    groups = iter(np.array_split(np.arange(CMB_TB), PLE_CHUNKS))

    def issue_group():
        for row in next(groups):
            fetch_row(dnext_ref, 1 - slot, int(row))

    x_lo = x_ref[:, :HALF]
    x_hi = x_ref[:, HALF:]
    for k in range(TOP_K):
        lo, hi = _unpack_pair(buf[slot, k])
        w = wt_ref[:, k:k + 1]
        x_lo = x_lo + w * lo
        x_hi = x_hi + w * hi

    ms = (jnp.sum(x_lo * x_lo, axis=-1, keepdims=True) + jnp.sum(x_hi * x_hi, axis=-1, keepdims=True)) / D_MODEL
    r = lax.rsqrt(ms + EPS)
    h_lo = (x_lo * r * g_ref[:, :HALF]).astype(BF16)
    h_hi = (x_hi * r * g_ref[:, HALF:]).astype(BF16)

    cols_per_chunk = D_MODEL // PLE_CHUNKS
    gates = []
    for c in range(PLE_CHUNKS):
        issue_group()
        cols = slice(c * cols_per_chunk, (c + 1) * cols_per_chunk)
        gates.append(jax.nn.sigmoid(jnp.dot(h_lo, wg_ref[:HALF, cols], preferred_element_type=F32)
                                    + jnp.dot(h_hi, wg_ref[HALF:, cols], preferred_element_type=F32)))
    gate = jnp.concatenate(gates, axis=1)

    def finish(p_ref, o_ref):
        pp = jnp.dot(p_ref[...].astype(BF16), wp_ref[...], preferred_element_type=F32)
        o_ref[:, :HALF] = x_lo + gate[:, :HALF] * pp[:, :HALF]
        o_ref[:, HALF:] = x_hi + gate[:, HALF:] * pp[:, HALF:]

    @pl.when(i < n_prompt_blocks)
    def _():
        finish(pp_ref, op_ref)

    @pl.when(i >= n_prompt_blocks)
    def _():
        finish(ps_ref, os_ref)

    @pl.when(i == n - 1)
    def _():
        wait_slot(1 - slot)


def _combine_ple(dest_blocks, x1, wgt_t, ys, g_ple, p_p, p_s, w_gate, w_proj):
    t = x1.shape[0]
    nb = t // CMB_TB
    npb = p_p.shape[0] // CMB_TB
    const = lambda i: (0, 0)
    prompt_rows = lambda i: (jnp.minimum(i, npb - 1), 0)
    sample_rows = lambda i: (jnp.maximum(i - npb, 0), 0)
    return pl.pallas_call(
        functools.partial(_combine_ple_kernel, n_prompt_blocks=npb),
        grid=(nb,),
        in_specs=[
            pl.BlockSpec((1, 1, TOP_K * CMB_TB), lambda i: (i, 0, 0), memory_space=pltpu.SMEM),
            pl.BlockSpec((1, 1, TOP_K * CMB_TB), lambda i: (jnp.minimum(i + 1, nb - 1), 0, 0),
                         memory_space=pltpu.SMEM),
            pl.BlockSpec((CMB_TB, D_MODEL), lambda i: (i, 0)),
            pl.BlockSpec((CMB_TB, TOP_K), lambda i: (i, 0)),
            pl.BlockSpec(memory_space=pl.ANY),
            pl.BlockSpec((1, D_MODEL), const),
            pl.BlockSpec((CMB_TB, PLE_DIM), prompt_rows),
            pl.BlockSpec((CMB_TB, PLE_DIM), sample_rows),
            pl.BlockSpec((D_MODEL, D_MODEL), const),
            pl.BlockSpec((PLE_DIM, D_MODEL), const),
        ],
        out_specs=[
            pl.BlockSpec((CMB_TB, D_MODEL), prompt_rows),
            pl.BlockSpec((CMB_TB, D_MODEL), sample_rows),
        ],
        out_shape=[
            jax.ShapeDtypeStruct((p_p.shape[0], D_MODEL), F32),
            jax.ShapeDtypeStruct((p_s.shape[0], D_MODEL), F32),
        ],
        scratch_shapes=[
            pltpu.VMEM((2, TOP_K, CMB_TB, HALF), U32),
            pltpu.SemaphoreType.DMA((2,)),
        ],
        compiler_params=_params(("arbitrary",)),
        name="moe_combine_ple",
    )(dest_blocks, dest_blocks, x1, wgt_t, ys, g_ple, p_p, p_s, w_gate, w_proj)


def _rope_tables(max_len):
    half = HEAD_DIM // 2
    inv_freq = ROPE_THETA ** (-jnp.arange(half, dtype=F32) / half)
    ang = jnp.arange(max_len, dtype=F32)[:, None] * inv_freq[None, :]
    cos, sin = jnp.cos(ang), jnp.sin(ang)
    return jnp.concatenate([cos, cos], axis=-1), jnp.concatenate([-sin, sin], axis=-1)


def _head_matrices():
    col = np.arange(IN_NC)
    same_head = (col[:, None] // HEAD_DIM) == (col[None, :] // HEAD_DIM)
    head_sum = np.where(same_head, 1.0 / HEAD_DIM, 0.0)
    src = (col // HEAD_DIM) * HEAD_DIM + (col % HEAD_DIM + HEAD_DIM // 2) % HEAD_DIM
    head_rot = (col[:, None] == src[None, :]).astype(np.float32)
    return jnp.asarray(head_sum, BF16), jnp.asarray(head_rot, BF16)


def _seq_block_tables(seq_lens):
    pos, first, last = [], [], []
    for s in seq_lens:
        pos += list(range(s // IN_TM))
        nb = s // ATT_QB
        first += [1] + [0] * (nb - 1)
        last += [0] * (nb - 1) + [1]
    return jnp.array(pos, jnp.int32), jnp.array(first, jnp.int32), jnp.array(last, jnp.int32)


def _layer(x_p, x_s, p_p, p_s, seq_lens, g_mix, w_in, q_norm, k_norm, sinks, g_sg_v, w_s, b_s, w_branch, w_o,
           g_ffn, w_router, b_router, w_gu, b_gu, w_down, b_down, g_ple, w_ple_gate, w_ple_proj):
    t = x_p.shape[0] + x_s.shape[0]
    w_in_r = w_in.astype(BF16)
    cos_t, sin_t = _rope_tables(max(seq_lens))
    head_sum, head_rot = _head_matrices()
    pos_blk, first, last = _seq_block_tables(seq_lens)
    reps = IN_NC // HEAD_DIM
    q_gain = jnp.tile(q_norm * (HEAD_DIM ** -0.5 * LOG2E), reps)[None, :]
    k_gain = jnp.tile(k_norm, reps)[None, :]

    proj = _in_proj(pos_blk, x_p, x_s, g_mix[None, :], w_in_r, q_gain, k_gain, cos_t, sin_t, head_sum, head_rot)
    attn = _attention(proj, sinks * LOG2E, first, last)
    bs_b = jnp.broadcast_to(b_s[:, :, None], (SG_GROUPS, SG_CHUNK, SG_CHUNK))
    merged = _mix(attn, proj, w_branch[0].astype(BF16), w_branch[1].astype(BF16), w_s.astype(BF16), bs_b,
                  g_sg_v[None, :])
    x1 = _out_proj(merged, w_o.astype(BF16), x_p, x_s)

    idx, wgt, rank, cnt = _router(x1, g_ffn[None, :], w_router.T, b_router[:, None])

    counts = cnt[:, 0].astype(jnp.int32)
    padded = (counts + MOE_BM - 1) // MOE_BM * MOE_BM
    padded_end = jnp.cumsum(padded)
    padded_start = padded_end - padded
    experts = jnp.arange(N_EXPERTS, dtype=jnp.int32)
    dest = rank + jnp.sum(jnp.where(idx[:, :, None] == experts, padded_start, 0), axis=-1)
    n_blocks = (t * TOP_K) // MOE_BM + N_EXPERTS
    block_row = jnp.arange(n_blocks, dtype=jnp.int32) * MOE_BM
    n_active = padded_end[-1] // MOE_BM
    block_e = jnp.minimum(jnp.sum(padded_end[None, :] <= block_row[:, None], axis=1), N_EXPERTS - 1)
    block_valid = jnp.clip(padded_start[block_e] + counts[block_e] - block_row, 0, MOE_BM)
    block_nsub = ((block_valid + MOE_SUB - 1) // MOE_SUB).astype(jnp.int32)
    block_e = jnp.where(block_row < padded_end[-1], block_e, block_e[n_active - 1]).astype(jnp.int32)

    def token_blocks(tb):
        return dest.reshape(TOP_K, t // tb, tb).transpose(1, 0, 2).reshape(t // tb, 1, TOP_K * tb)

    xs = _dispatch(padded_start + counts, (-counts) % MOE_SUB, token_blocks(DSP_TB), x1, g_ffn[None, :],
                   n_blocks * MOE_BM)
    ys = _moe(block_e, block_nsub, n_active[None].astype(jnp.int32), xs, w_gu, b_gu[:, None, :], w_down,
              b_down[:, None, :])
    return _combine_ple(token_blocks(CMB_TB), x1, wgt.T, ys, g_ple[None, :], p_p, p_s,
                        w_ple_gate.astype(BF16), w_ple_proj.astype(BF16))


def kernel(x_prompt, x_sample, p_prompt, p_sample, g_mix, w_in, q_norm, k_norm, sinks, g_sg_v, w_s, b_s,
           w_branch, w_o, g_ffn, w_router, b_router, w_gu, b_gu, w_down, b_down, g_ple, w_ple_gate,
           w_ple_proj):
    depth = g_mix.shape[0]
    bp, sp, d = x_prompt.shape
    bs, ss, _ = x_sample.shape
    seq_lens = [sp] * bp + [ss] * bs
    x_p = x_prompt.reshape(bp * sp, d)
    x_s = x_sample.reshape(bs * ss, d)
    for l in range(depth):
        x_p, x_s = _layer(x_p, x_s, p_prompt[l].reshape(bp * sp, PLE_DIM), p_sample[l].reshape(bs * ss, PLE_DIM),
                          seq_lens, g_mix[l], w_in[l], q_norm[l], k_norm[l], sinks[l], g_sg_v[l], w_s[l], b_s[l],
                          w_branch[l], w_o[l], g_ffn[l], w_router[l], b_router[l], w_gu[l], b_gu[l], w_down[l],
                          b_down[l], g_ple[l], w_ple_gate[l], w_ple_proj[l])
    return x_p.reshape(bp, sp, d), x_s.reshape(bs, ss, d)
```
